```python
import jax, jax.numpy as jnp
from jax import lax
import numpy as np

D_MODEL = 1024
BATCH = 16
SEQ = 2048
DEPTH = 2

HEAD_DIM = 64
N_ATTN_HEADS = D_MODEL // (2 * HEAD_DIM)
N_GMLP_GROUPS = D_MODEL // (2 * HEAD_DIM)
ATTN_WIDTH = N_ATTN_HEADS * HEAD_DIM
GMLP_WIDTH = N_GMLP_GROUPS * HEAD_DIM
MIX_WIDTH = ATTN_WIDTH + GMLP_WIDTH
IN_PROJ_WIDTH = 3 * ATTN_WIDTH + 2 * GMLP_WIDTH
MOBA_BLOCK = 256
MOBA_TOP_K = 3
QUERY_CHUNK = 16
GMLP_CHUNK = 128
D_FF = ((8 * D_MODEL // 3 + 255) // 256) * 256
PLE_DIM = 256
RMS_EPS = 1e-6
LN_EPS = 1e-5
NEG_INF = -1e30

kernel_name = "hybrid_moba_gmlp_macaron_trunk"


def rms_norm(x, g):
    xf = x.astype(jnp.float32)
    y = xf * lax.rsqrt(jnp.mean(xf * xf, axis=-1, keepdims=True) + RMS_EPS)
    return (y * g.astype(jnp.float32)).astype(x.dtype)


def swiglu_ffn(x, w_gate, w_up, w_down):
    return (jax.nn.silu(x @ w_gate) * (x @ w_up)) @ w_down


def alibi_slopes(n_heads):
    start = 2.0 ** (-8.0 / n_heads)
    return jnp.asarray(np.array([start ** (i + 1) for i in range(n_heads)], dtype=np.float32))


def moba_attention(q, k, v):
    B, H, S, dh = q.shape
    nb = -(-S // MOBA_BLOCK)
    pad = nb * MOBA_BLOCK - S
    padding = ((0, 0), (0, 0), (0, pad), (0, 0))
    k_blocks = jnp.pad(k, padding).reshape(B, H, nb, MOBA_BLOCK, dh)
    v_blocks = jnp.pad(v, padding).reshape(B, H, nb, MOBA_BLOCK, dh)
    slopes = alibi_slopes(H)
    scale = dh ** -0.5

    k_mean = jnp.mean(k_blocks.astype(jnp.float32), axis=3)
    gate = jnp.einsum("bhsd,bhnd->bhsn", q.astype(jnp.float32), k_mean)
    pos = jnp.arange(S)
    fully_past = jnp.arange(nb)[None, :] < (pos // MOBA_BLOCK)[:, None]
    gate = jnp.where(fully_past[None, None], gate, NEG_INF)
    n_sel = min(MOBA_TOP_K, nb)
    top_vals, top_idx = lax.top_k(gate, n_sel)
    sel_valid = top_vals > 0.5 * NEG_INF

    nqc = S // QUERY_CHUNK

    def to_chunks(a):
        a = a.reshape((B, H, nqc, QUERY_CHUNK) + a.shape[3:])
        return jnp.moveaxis(a, 2, 0)

    b_ix = jnp.arange(B)[:, None, None, None]
    h_ix = jnp.arange(H)[None, :, None, None]
    blk_off = jnp.arange(MOBA_BLOCK)
    n_sel_keys = n_sel * MOBA_BLOCK

    def one_chunk(args):
        q_c, idx_c, valid_c, c = args
        t = c * QUERY_CHUNK + jnp.arange(QUERY_CHUNK)
        own = (c * QUERY_CHUNK) // MOBA_BLOCK
        k_own = lax.dynamic_index_in_dim(k_blocks, own, axis=2, keepdims=False)
        v_own = lax.dynamic_index_in_dim(v_blocks, own, axis=2, keepdims=False)
        s_own = own * MOBA_BLOCK + blk_off
        l_own = jnp.einsum("bhqd,bhkd->bhqk", q_c, k_own).astype(jnp.float32) * scale
        d_own = (t[:, None] - s_own[None, :]).astype(jnp.float32)
        l_own = l_own - slopes[None, :, None, None] * d_own[None, None]
        l_own = jnp.where((s_own[None, :] <= t[:, None])[None, None], l_own, NEG_INF)
        k_sel = k_blocks[b_ix, h_ix, idx_c]
        v_sel = v_blocks[b_ix, h_ix, idx_c]
        l_sel = jnp.einsum("bhqd,bhqnkd->bhqnk", q_c, k_sel).astype(jnp.float32) * scale
        s_sel = idx_c[..., None] * MOBA_BLOCK + blk_off
        d_sel = (t[None, None, :, None, None] - s_sel).astype(jnp.float32)
        l_sel = l_sel - slopes[None, :, None, None, None] * d_sel
        l_sel = jnp.where(valid_c[..., None], l_sel, NEG_INF)
        logits = jnp.concatenate([l_sel.reshape(B, H, QUERY_CHUNK, n_sel_keys), l_own], axis=-1)
        probs = jax.nn.softmax(logits, axis=-1).astype(v.dtype)
        p_sel = probs[..., :n_sel_keys].reshape(B, H, QUERY_CHUNK, n_sel, MOBA_BLOCK)
        p_own = probs[..., n_sel_keys:]
        return (jnp.einsum("bhqnk,bhqnkd->bhqd", p_sel, v_sel)
                + jnp.einsum("bhqk,bhkd->bhqd", p_own, v_own))

    out = lax.map(one_chunk, (to_chunks(q), to_chunks(top_idx), to_chunks(sel_valid),
                              jnp.arange(nqc)))
    return jnp.moveaxis(out, 0, 2).reshape(B, H, S, dh)


def chunked_spatial_gating(u, v, ln_g, ln_b, w_s, b_s):
    B, S, G, dg = v.shape
    vf = v.astype(jnp.float32)
    mu = jnp.mean(vf, axis=-1, keepdims=True)
    var = jnp.mean(jnp.square(vf - mu), axis=-1, keepdims=True)
    vn = ((vf - mu) * lax.rsqrt(var + LN_EPS) * ln_g.reshape(G, dg).astype(jnp.float32)
          + ln_b.reshape(G, dg).astype(jnp.float32)).astype(v.dtype)
    nc = S // GMLP_CHUNK
    causal = jnp.tril(jnp.ones((GMLP_CHUNK, GMLP_CHUNK), dtype=w_s.dtype))
    mixed = jnp.einsum("gts,bcsgd->bctgd", w_s * causal,
                       vn.reshape(B, nc, GMLP_CHUNK, G, dg))
    mixed = mixed + b_s.T[None, None, :, :, None]
    return u * mixed.reshape(B, S, G, dg)


def token_mixer(hn, w_in, gmlp_ln_g, gmlp_ln_b, gmlp_w_s, gmlp_b_s,
                attn_out_norm, gmlp_out_norm, w_out):
    B, S, _ = hn.shape
    z = hn @ w_in
    q, k, v, gu, gv = jnp.split(
        z, [ATTN_WIDTH, 2 * ATTN_WIDTH, 3 * ATTN_WIDTH, 3 * ATTN_WIDTH + GMLP_WIDTH], axis=-1)

    def heads(a):
        return a.reshape(B, S, N_ATTN_HEADS, HEAD_DIM).transpose(0, 2, 1, 3)

    attn = moba_attention(heads(q), heads(k), heads(v))
    attn = attn.transpose(0, 2, 1, 3).reshape(B, S, ATTN_WIDTH)

    gu = jax.nn.gelu(gu, approximate=False).reshape(B, S, N_GMLP_GROUPS, HEAD_DIM)
    gv = jax.nn.gelu(gv, approximate=False).reshape(B, S, N_GMLP_GROUPS, HEAD_DIM)
    g = chunked_spatial_gating(gu, gv, gmlp_ln_g, gmlp_ln_b, gmlp_w_s, gmlp_b_s)
    g = g.reshape(B, S, GMLP_WIDTH)

    merged = jnp.concatenate([rms_norm(attn, attn_out_norm), rms_norm(g, gmlp_out_norm)], axis=-1)
    return merged @ w_out


def setup_inputs(seed: int = 0) -> dict:
    key = jax.random.key(seed)
    ks = iter(jax.random.split(key, 32))

    def nrm(shape, scale):
        return jax.random.normal(next(ks), shape, jnp.float32) * scale

    def gain(n):
        return 1.0 + nrm((DEPTH, n), 0.05)

    return {
        "x": nrm((BATCH, SEQ, D_MODEL), 1.0),
        "p": nrm((DEPTH, BATCH, SEQ, PLE_DIM), 1.0),
        "ffn1_pre_norm": gain(D_MODEL),
        "ffn1_w_gate": nrm((DEPTH, D_MODEL, D_FF), D_MODEL ** -0.5),
        "ffn1_w_up": nrm((DEPTH, D_MODEL, D_FF), D_MODEL ** -0.5),
        "ffn1_w_down": nrm((DEPTH, D_FF, D_MODEL), D_FF ** -0.5),
        "ffn1_post_norm": gain(D_MODEL),
        "mix_pre_norm": gain(D_MODEL),
        "w_in": nrm((DEPTH, D_MODEL, IN_PROJ_WIDTH), D_MODEL ** -0.5),
        "gmlp_ln_g": gain(GMLP_WIDTH),
        "gmlp_ln_b": nrm((DEPTH, GMLP_WIDTH), 0.02),
        "gmlp_w_s": nrm((DEPTH, N_GMLP_GROUPS, GMLP_CHUNK, GMLP_CHUNK), GMLP_CHUNK ** -0.5),
        "gmlp_b_s": 1.0 + nrm((DEPTH, N_GMLP_GROUPS, GMLP_CHUNK), 0.1),
        "attn_out_norm": gain(ATTN_WIDTH),
        "gmlp_out_norm": gain(GMLP_WIDTH),
        "w_out": nrm((DEPTH, MIX_WIDTH, D_MODEL), MIX_WIDTH ** -0.5),
        "mix_post_norm": gain(D_MODEL),
        "ffn2_pre_norm": gain(D_MODEL),
        "ffn2_w_gate": nrm((DEPTH, D_MODEL, D_FF), D_MODEL ** -0.5),
        "ffn2_w_up": nrm((DEPTH, D_MODEL, D_FF), D_MODEL ** -0.5),
        "ffn2_w_down": nrm((DEPTH, D_FF, D_MODEL), D_FF ** -0.5),
        "ffn2_post_norm": gain(D_MODEL),
        "ple_pre_norm": gain(D_MODEL),
        "ple_w_gate": nrm((DEPTH, D_MODEL, D_MODEL), D_MODEL ** -0.5),
        "ple_w_proj": nrm((DEPTH, PLE_DIM, D_MODEL), PLE_DIM ** -0.5),
        "ple_post_norm": gain(D_MODEL),
    }


def reference(x, p,
              ffn1_pre_norm, ffn1_w_gate, ffn1_w_up, ffn1_w_down, ffn1_post_norm,
              mix_pre_norm, w_in, gmlp_ln_g, gmlp_ln_b, gmlp_w_s, gmlp_b_s,
              attn_out_norm, gmlp_out_norm, w_out, mix_post_norm,
              ffn2_pre_norm, ffn2_w_gate, ffn2_w_up, ffn2_w_down, ffn2_post_norm,
              ple_pre_norm, ple_w_gate, ple_w_proj, ple_post_norm):
    h = x
    for i in range(DEPTH):
        f1 = swiglu_ffn(rms_norm(h, ffn1_pre_norm[i]), ffn1_w_gate[i], ffn1_w_up[i], ffn1_w_down[i])
        h = h + 0.5 * rms_norm(f1, ffn1_post_norm[i])
        m = token_mixer(rms_norm(h, mix_pre_norm[i]), w_in[i], gmlp_ln_g[i], gmlp_ln_b[i],
                        gmlp_w_s[i], gmlp_b_s[i], attn_out_norm[i], gmlp_out_norm[i], w_out[i])
        h = h + rms_norm(m, mix_post_norm[i])
        f2 = swiglu_ffn(rms_norm(h, ffn2_pre_norm[i]), ffn2_w_gate[i], ffn2_w_up[i], ffn2_w_down[i])
        h = h + 0.5 * rms_norm(f2, ffn2_post_norm[i])
        gate = jax.nn.sigmoid(rms_norm(h, ple_pre_norm[i]) @ ple_w_gate[i])
        e = gate * (p[i] @ ple_w_proj[i])
        h = h + rms_norm(e, ple_post_norm[i])
    return h
```

```python
import functools

import numpy as np
import jax
import jax.numpy as jnp
from jax import lax
from jax.experimental import pallas as pl
from jax.experimental.pallas import tpu as pltpu

D_MODEL = 1024
BATCH = 16
SEQ = 2048
DEPTH = 2
HEAD_DIM = 64
N_HEADS = 8
N_GROUPS = 8
ATTN_W = N_HEADS * HEAD_DIM
GMLP_W = N_GROUPS * HEAD_DIM
BLK = 256
N_BLK = SEQ // BLK
TOP_K = 3
CHUNK = 128
D_FF = 2816
PLE_DIM = 256
RMS_EPS = 1e-6
LN_EPS = 1e-5
NEG = -1e30
TOKENS = BATCH * SEQ

LANES = 128
PAIR_W = 2 * HEAD_DIM
TM = 512
FF_CHUNK = 256
VMEM_LIMIT = 56 * 1024 * 1024

F32 = jnp.float32
BF16 = jnp.bfloat16


def _rms(x, g):
    return x * lax.rsqrt(jnp.mean(x * x, axis=-1, keepdims=True) + RMS_EPS) * g


def _gelu(x):
    return 0.5 * x * (1.0 + lax.erf(x * np.float32(np.sqrt(0.5))))


def _dot(a, b):
    return jnp.dot(a, b, preferred_element_type=F32)


def _dot_nt(a, b):
    return lax.dot_general(a, b, (((1,), (1,)), ((), ())), preferred_element_type=F32)


def _const_spec(shape):
    nd = len(shape)
    return pl.BlockSpec(shape, lambda *_: (0,) * nd, pipeline_mode=pl.Buffered(1))


def _params(n_axes):
    return pltpu.CompilerParams(
        dimension_semantics=("arbitrary",) * n_axes, vmem_limit_bytes=VMEM_LIMIT)


def _ffn_kernel(h_ref, gpre_ref, wg_ref, wu_ref, wd_ref, gpost_ref, o_ref, a_ref):
    h = h_ref[...]
    hn = _rms(h, gpre_ref[...]).astype(BF16)
    for c in range(D_FF // FF_CHUNK):
        sl = slice(c * FF_CHUNK, (c + 1) * FF_CHUNK)
        g = _dot(hn, wg_ref[:, sl])
        u = _dot(hn, wu_ref[:, sl])
        a_ref[:, sl] = (g * jax.nn.sigmoid(g) * u).astype(BF16)
    f = _dot(a_ref[...], wd_ref[...])
    o_ref[...] = h + 0.5 * _rms(f, gpost_ref[...])


def _ffn(h, gpre, wg, wu, wd, gpost):
    row = pl.BlockSpec((TM, D_MODEL), lambda i: (i, 0))
    return pl.pallas_call(
        _ffn_kernel,
        grid=(TOKENS // TM,),
        in_specs=[row, _const_spec((1, D_MODEL)), _const_spec((D_MODEL, D_FF)),
                  _const_spec((D_MODEL, D_FF)), _const_spec((D_FF, D_MODEL)),
                  _const_spec((1, D_MODEL))],
        out_specs=row,
        out_shape=jax.ShapeDtypeStruct((TOKENS, D_MODEL), F32),
        scratch_shapes=[pltpu.VMEM((TM, D_FF), BF16)],
        compiler_params=_params(1),
        name="ffn",
    )(h, gpre, wg, wu, wd, gpost)


def _inproj_kernel(h_ref, g_ref, wqk_ref, wt_ref, q_ref, k_ref, vt_ref, gut_ref, gvt_ref):
    hn = _rms(h_ref[...], g_ref[...]).astype(BF16)
    qk = _dot(hn, wqk_ref[...])
    q_ref[...] = qk[:, :ATTN_W].astype(BF16)
    k_ref[...] = qk[:, ATTN_W:].astype(BF16)
    zt = _dot_nt(wt_ref[...], hn)
    vt_ref[0] = zt[:ATTN_W].astype(BF16)
    gut_ref[0] = _gelu(zt[ATTN_W:ATTN_W + GMLP_W]).astype(BF16)
    gvt_ref[0] = _gelu(zt[ATTN_W + GMLP_W:]).astype(BF16)


def _inproj(h, g, wqk, wt):
    tiles_per_seq = SEQ // TM
    row = pl.BlockSpec((TM, D_MODEL), lambda i: (i, 0))
    tok = pl.BlockSpec((TM, ATTN_W), lambda i: (i, 0))
    feat = pl.BlockSpec((1, ATTN_W, TM), lambda i: (i // tiles_per_seq, 0, i % tiles_per_seq))
    tok_shape = jax.ShapeDtypeStruct((TOKENS, ATTN_W), BF16)
    feat_shape = jax.ShapeDtypeStruct((BATCH, ATTN_W, SEQ), BF16)
    return pl.pallas_call(
        _inproj_kernel,
        grid=(TOKENS // TM,),
        in_specs=[row, _const_spec((1, D_MODEL)), _const_spec((D_MODEL, 2 * ATTN_W)),
                  _const_spec((ATTN_W + 2 * GMLP_W, D_MODEL))],
        out_specs=[tok, tok, feat, feat, feat],
        out_shape=[tok_shape, tok_shape, feat_shape, feat_shape, feat_shape],
        compiler_params=_params(1),
        name="inproj",
    )(h, g, wqk, wt)


def _attn_kernel(slope_ref, q_ref, k_ref, vt_ref, o_ref, kaug_ref, bias_ref):
    pair = pl.program_id(1)
    lane = lax.broadcasted_iota(jnp.int32, (1, PAIR_W), 1)
    k_all = k_ref[...]
    q_all = q_ref[...]
    key_pos = (lax.broadcasted_iota(jnp.int32, (SEQ, PAIR_W), 0) & (BLK - 1)).astype(F32)
    k_mean = jnp.mean(k_all.astype(F32).reshape(N_BLK, BLK, PAIR_W), axis=1)
    blk_of_row = lax.broadcasted_iota(jnp.int32, (N_BLK, SEQ), 0)
    blk_of_query = lax.broadcasted_iota(jnp.int32, (N_BLK, SEQ), 1) // BLK
    slopes = []
    head_masks = []
    for a in range(2):
        slope = slope_ref[pl.ds(2 * pair + a, 1), :]
        in_head = (lane >= a * HEAD_DIM) & (lane < (a + 1) * HEAD_DIM)
        aux = (1 - a) * HEAD_DIM
        slopes.append(slope)
        head_masks.append((in_head, aux))
        kaug_ref[a] = jnp.where(
            in_head, k_all,
            jnp.where(lane == aux, 1.0,
                      jnp.where(lane == aux + 1, slope * key_pos, 0.0)).astype(BF16))
        km = jnp.where(in_head, k_mean, 0.0)
        km_hi = km.astype(BF16)
        km_lo = (km - km_hi.astype(F32)).astype(BF16)
        gate = _dot_nt(km_hi, q_all) + _dot_nt(km_lo, q_all)
        gate = jnp.where(blk_of_row < blk_of_query, gate, NEG)
        rank = jnp.zeros((N_BLK, SEQ), jnp.int32)
        for other in range(N_BLK):
            row = gate[other:other + 1, :]
            beats = (row > gate) | ((row == gate) & (other < blk_of_row))
            rank = rank + beats.astype(jnp.int32)
        chosen = (rank < TOP_K) & (gate > 0.5 * NEG)
        offset = ((blk_of_query - blk_of_row) * BLK).astype(F32)
        bias_ref[a] = jnp.where(chosen, -slope[:, :1] * offset, NEG)

    key_idx = lax.broadcasted_iota(jnp.int32, (BLK, BLK), 0)
    query_idx = lax.broadcasted_iota(jnp.int32, (BLK, BLK), 1)
    causal = key_idx <= query_idx
    q_pos = lax.broadcasted_iota(jnp.int32, (BLK, PAIR_W), 0).astype(F32)
    for i in range(N_BLK):
        rows = slice(i * BLK, (i + 1) * BLK)
        q_tile = q_ref[rows, :].astype(F32)
        outs = []
        for a in range(2):
            in_head, aux = head_masks[a]
            q_aug = jnp.where(
                in_head, q_tile * (HEAD_DIM ** -0.5),
                jnp.where(lane == aux, -slopes[a] * q_pos,
                          jnp.where(lane == aux + 1, 1.0, 0.0))).astype(BF16)
            feat = slice(a * HEAD_DIM, (a + 1) * HEAD_DIM)
            s = _dot_nt(kaug_ref[a, rows, :], q_aug)
            s = jnp.where(causal, s, NEG)
            m = jnp.max(s, axis=0, keepdims=True)
            p = jnp.exp(s - m)
            l = jnp.sum(p, axis=0, keepdims=True)
            acc = _dot(vt_ref[0, feat, rows], p.astype(BF16))
            for j in range(i):
                cols = slice(j * BLK, (j + 1) * BLK)
                s = _dot_nt(kaug_ref[a, cols, :], q_aug) + bias_ref[a, j:j + 1, rows]
                m_new = jnp.maximum(m, jnp.max(s, axis=0, keepdims=True))
                alpha = jnp.exp(m - m_new)
                p = jnp.exp(s - m_new)
                l = alpha * l + jnp.sum(p, axis=0, keepdims=True)
                acc = alpha * acc + _dot(vt_ref[0, feat, cols], p.astype(BF16))
                m = m_new
            outs.append(acc / l)
        o_ref[rows, :] = jnp.concatenate(outs, axis=0).T.astype(BF16)


def _attention(slopes, q, k, vt):
    tok = pl.BlockSpec((SEQ, PAIR_W), lambda b, p: (b, p))
    return pl.pallas_call(
        _attn_kernel,
        grid=(BATCH, ATTN_W // PAIR_W),
        in_specs=[_const_spec((N_HEADS, PAIR_W)), tok, tok,
                  pl.BlockSpec((1, PAIR_W, SEQ), lambda b, p: (b, p, 0))],
        out_specs=tok,
        out_shape=jax.ShapeDtypeStruct((TOKENS, ATTN_W), BF16),
        scratch_shapes=[pltpu.VMEM((2, SEQ, PAIR_W), BF16),
                        pltpu.VMEM((2, N_BLK, SEQ), F32)],
        compiler_params=_params(2),
        name="moba_attention",
    )(slopes, q, k, vt)


def _mixout_kernel(h_ref, attn_ref, gut_ref, gvt_ref, lng_ref, lnb_ref, ws_ref, bs_ref,
                   an_ref, gn_ref, woa_ref, wog_ref, gpost_ref, o_ref, vn_ref, gated_ref):
    n_chunks = TM // CHUNK
    for c in range(n_chunks):
        cols = slice(c * CHUNK, (c + 1) * CHUNK)
        gv = gvt_ref[0, :, cols].astype(F32).reshape(N_GROUPS, HEAD_DIM, CHUNK)
        mu = jnp.mean(gv, axis=1, keepdims=True)
        var = jnp.mean(jnp.square(gv - mu), axis=1, keepdims=True)
        vn = ((gv - mu) * lax.rsqrt(var + LN_EPS)).reshape(GMLP_W, CHUNK)
        vn = (vn * lng_ref[...] + lnb_ref[...]).astype(BF16).reshape(N_GROUPS, HEAD_DIM, CHUNK)
        vn_ref[:, c * HEAD_DIM:(c + 1) * HEAD_DIM, :] = vn
    t_idx = lax.broadcasted_iota(jnp.int32, (CHUNK, CHUNK), 0)
    s_idx = lax.broadcasted_iota(jnp.int32, (CHUNK, CHUNK), 1)
    ssq = jnp.zeros((n_chunks, 1, CHUNK), F32)
    for g in range(N_GROUPS):
        w = jnp.where(s_idx <= t_idx, ws_ref[g], 0.0).astype(BF16)
        mixed = _dot_nt(vn_ref[g], w) + bs_ref[g:g + 1, :]
        feat = slice(g * HEAD_DIM, (g + 1) * HEAD_DIM)
        gu = jnp.concatenate(
            [gut_ref[0, feat, c * CHUNK:(c + 1) * CHUNK] for c in range(n_chunks)], axis=0)
        gated = (gu.astype(F32) * mixed).reshape(n_chunks, HEAD_DIM, CHUNK)
        ssq = ssq + jnp.sum(gated * gated, axis=1, keepdims=True)
        gated_ref[:, feat, :] = gated
    inv = lax.rsqrt(ssq * (1.0 / GMLP_W) + RMS_EPS)
    attn_n = _rms(attn_ref[...].astype(F32), an_ref[...]).astype(BF16)
    gmlp_n = jnp.concatenate(
        [(gated_ref[c] * inv[c] * gn_ref[...]).T for c in range(n_chunks)], axis=0).astype(BF16)
    m = _dot(attn_n, woa_ref[...]) + _dot(gmlp_n, wog_ref[...])
    o_ref[...] = h_ref[...] + _rms(m, gpost_ref[...])


def _mixout(h, attn, gut, gvt, lng, lnb, ws, bs, an, gn, woa, wog, gpost):
    tiles_per_seq = SEQ // TM
    row = pl.BlockSpec((TM, D_MODEL), lambda i: (i, 0))
    feat = pl.BlockSpec((1, GMLP_W, TM), lambda i: (i // tiles_per_seq, 0, i % tiles_per_seq))
    return pl.pallas_call(
        _mixout_kernel,
        grid=(TOKENS // TM,),
        in_specs=[row, pl.BlockSpec((TM, ATTN_W), lambda i: (i, 0)), feat, feat,
                  _const_spec((GMLP_W, CHUNK)), _const_spec((GMLP_W, CHUNK)),
                  _const_spec((N_GROUPS, CHUNK, CHUNK)), _const_spec((N_GROUPS, CHUNK)),
                  _const_spec((1, ATTN_W)), _const_spec((GMLP_W, CHUNK)),
                  _const_spec((ATTN_W, D_MODEL)), _const_spec((GMLP_W, D_MODEL)),
                  _const_spec((1, D_MODEL))],
        out_specs=row,
        out_shape=jax.ShapeDtypeStruct((TOKENS, D_MODEL), F32),
        scratch_shapes=[pltpu.VMEM((N_GROUPS, (TM // CHUNK) * HEAD_DIM, CHUNK), BF16),
                        pltpu.VMEM((TM // CHUNK, GMLP_W, CHUNK), F32)],
        compiler_params=_params(1),
        name="mixout",
    )(h, attn, gut, gvt, lng, lnb, ws, bs, an, gn, woa, wog, gpost)


def _ple_kernel(h_ref, p_ref, gpre_ref, wg_ref, wp_ref, gpost_ref, o_ref):
    h = h_ref[...]
    hn = _rms(h, gpre_ref[...]).astype(BF16)
    gate = jax.nn.sigmoid(_dot(hn, wg_ref[...]))
    e = gate * _dot(p_ref[...].astype(BF16), wp_ref[...])
    o_ref[...] = h + _rms(e, gpost_ref[...])


def _ple(h, p, gpre, wg, wp, gpost):
    row = pl.BlockSpec((TM, D_MODEL), lambda i: (i, 0))
    return pl.pallas_call(
        _ple_kernel,
        grid=(TOKENS // TM,),
        in_specs=[row, pl.BlockSpec((TM, PLE_DIM), lambda i: (i, 0)),
                  _const_spec((1, D_MODEL)), _const_spec((D_MODEL, D_MODEL)),
                  _const_spec((PLE_DIM, D_MODEL)), _const_spec((1, D_MODEL))],
        out_specs=row,
        out_shape=jax.ShapeDtypeStruct((TOKENS, D_MODEL), F32),
        compiler_params=_params(1),
        name="ple",
    )(h, p, gpre, wg, wp, gpost)


def _alibi_slopes():
    start = 2.0 ** (-8.0 / N_HEADS)
    s = np.array([start ** (i + 1) for i in range(N_HEADS)], dtype=np.float32)
    return jnp.asarray(np.broadcast_to(s[:, None], (N_HEADS, PAIR_W)).copy())


def kernel(x, p, ffn1_pre_norm, ffn1_w_gate, ffn1_w_up, ffn1_w_down, ffn1_post_norm, mix_pre_norm, w_in, gmlp_ln_g, gmlp_ln_b, gmlp_w_s, gmlp_b_s, attn_out_norm, gmlp_out_norm, w_out, mix_post_norm, ffn2_pre_norm, ffn2_w_gate, ffn2_w_up, ffn2_w_down, ffn2_post_norm, ple_pre_norm, ple_w_gate, ple_w_proj, ple_post_norm):
    assert x.shape == (BATCH, SEQ, D_MODEL) and p.shape == (DEPTH, BATCH, SEQ, PLE_DIM)
    slopes = _alibi_slopes()
    h = x.reshape(TOKENS, D_MODEL)
    p = p.reshape(DEPTH, TOKENS, PLE_DIM)

    def vec(v):
        return v.reshape(1, -1)

    def col(v):
        return jnp.broadcast_to(v.reshape(-1, 1), (v.shape[0], CHUNK))

    for i in range(DEPTH):
        h = _ffn(h, vec(ffn1_pre_norm[i]), ffn1_w_gate[i].astype(BF16), ffn1_w_up[i].astype(BF16),
                 ffn1_w_down[i].astype(BF16), vec(ffn1_post_norm[i]))
        w_qk = w_in[i, :, :2 * ATTN_W].astype(BF16)
        w_t = w_in[i, :, 2 * ATTN_W:].T.astype(BF16)
        q, k, vt, gut, gvt = _inproj(h, vec(mix_pre_norm[i]), w_qk, w_t)
        attn = _attention(slopes, q, k, vt)
        h = _mixout(h, attn, gut, gvt, col(gmlp_ln_g[i]), col(gmlp_ln_b[i]), gmlp_w_s[i],
                    gmlp_b_s[i], vec(attn_out_norm[i]), col(gmlp_out_norm[i]),
                    w_out[i, :ATTN_W].astype(BF16), w_out[i, ATTN_W:].astype(BF16),
                    vec(mix_post_norm[i]))
        h = _ffn(h, vec(ffn2_pre_norm[i]), ffn2_w_gate[i].astype(BF16), ffn2_w_up[i].astype(BF16),
                 ffn2_w_down[i].astype(BF16), vec(ffn2_post_norm[i]))
        h = _ple(h, p[i], vec(ple_pre_norm[i]), ple_w_gate[i].astype(BF16),
                 ple_w_proj[i].astype(BF16), vec(ple_post_norm[i]))
    return h.reshape(BATCH, SEQ, D_MODEL)
```

```python
import functools

import numpy as np
import jax
import jax.numpy as jnp
from jax import lax
from jax.experimental import pallas as pl
from jax.experimental.pallas import tpu as pltpu

D_MODEL = 1024
BATCH = 16
SEQ = 2048
DEPTH = 2
HEAD_DIM = 64
N_HEADS = 8
N_GROUPS = 8
ATTN_W = N_HEADS * HEAD_DIM
GMLP_W = N_GROUPS * HEAD_DIM
BLK = 256
N_BLK = SEQ // BLK
TOP_K = 3
CHUNK = 128
D_FF = 2816
PLE_DIM = 256
RMS_EPS = 1e-6
LN_EPS = 1e-5
NEG = -1e30
TOKENS = BATCH * SEQ

LANES = 128
PAIR_W = 2 * HEAD_DIM
TM = 512
FF_CHUNK = 256
VMEM_LIMIT = 56 * 1024 * 1024

F32 = jnp.float32
BF16 = jnp.bfloat16


def _rms(x, g):
    return x * lax.rsqrt(jnp.mean(x * x, axis=-1, keepdims=True) + RMS_EPS) * g


def _gelu(x):
    return 0.5 * x * (1.0 + lax.erf(x * np.float32(np.sqrt(0.5))))


def _dot(a, b):
    return jnp.dot(a, b, preferred_element_type=F32)


def _dot_nt(a, b):
    return lax.dot_general(a, b, (((1,), (1,)), ((), ())), preferred_element_type=F32)


def _const_spec(shape):
    nd = len(shape)
    return pl.BlockSpec(shape, lambda *_: (0,) * nd, pipeline_mode=pl.Buffered(1))


def _params(n_axes):
    return pltpu.CompilerParams(
        dimension_semantics=("arbitrary",) * n_axes, vmem_limit_bytes=VMEM_LIMIT)


def _ffn_kernel(h_ref, gpre_ref, wg_ref, wu_ref, wd_ref, gpost_ref, o_ref, a_ref):
    h = h_ref[...]
    hn = _rms(h, gpre_ref[...]).astype(BF16)
    for c in range(D_FF // FF_CHUNK):
        sl = slice(c * FF_CHUNK, (c + 1) * FF_CHUNK)
        g = _dot(hn, wg_ref[:, sl])
        u = _dot(hn, wu_ref[:, sl])
        a_ref[:, sl] = (g * jax.nn.sigmoid(g) * u).astype(BF16)
    f = _dot(a_ref[...], wd_ref[...])
    o_ref[...] = h + 0.5 * _rms(f, gpost_ref[...])


def _ffn(h, gpre, wg, wu, wd, gpost):
    row = pl.BlockSpec((TM, D_MODEL), lambda i: (i, 0))
    return pl.pallas_call(
        _ffn_kernel,
        grid=(TOKENS // TM,),
        in_specs=[row, _const_spec((1, D_MODEL)), _const_spec((D_MODEL, D_FF)),
                  _const_spec((D_MODEL, D_FF)), _const_spec((D_FF, D_MODEL)),
                  _const_spec((1, D_MODEL))],
        out_specs=row,
        out_shape=jax.ShapeDtypeStruct((TOKENS, D_MODEL), F32),
        scratch_shapes=[pltpu.VMEM((TM, D_FF), BF16)],
        compiler_params=_params(1),
        name="ffn",
    )(h, gpre, wg, wu, wd, gpost)


def _inproj_kernel(h_ref, g_ref, wk_ref, wt_ref, k_ref, qt_ref, vt_ref, gut_ref, gvt_ref):
    hn = _rms(h_ref[...], g_ref[...]).astype(BF16)
    k_ref[...] = _dot(hn, wk_ref[...]).astype(BF16)
    zt = _dot_nt(wt_ref[...], hn)
    qt_ref[0] = zt[:ATTN_W].astype(BF16)
    vt_ref[0] = zt[ATTN_W:2 * ATTN_W].astype(BF16)
    gut_ref[0] = _gelu(zt[2 * ATTN_W:2 * ATTN_W + GMLP_W]).astype(BF16)
    gvt_ref[0] = _gelu(zt[2 * ATTN_W + GMLP_W:]).astype(BF16)


def _inproj(h, g, wk, wt):
    tiles_per_seq = SEQ // TM
    row = pl.BlockSpec((TM, D_MODEL), lambda i: (i, 0))
    tok = pl.BlockSpec((TM, ATTN_W), lambda i: (i, 0))
    feat = pl.BlockSpec((1, ATTN_W, TM), lambda i: (i // tiles_per_seq, 0, i % tiles_per_seq))
    tok_shape = jax.ShapeDtypeStruct((TOKENS, ATTN_W), BF16)
    feat_shape = jax.ShapeDtypeStruct((BATCH, ATTN_W, SEQ), BF16)
    return pl.pallas_call(
        _inproj_kernel,
        grid=(TOKENS // TM,),
        in_specs=[row, _const_spec((1, D_MODEL)), _const_spec((D_MODEL, ATTN_W)),
                  _const_spec((2 * ATTN_W + 2 * GMLP_W, D_MODEL))],
        out_specs=[tok, feat, feat, feat, feat],
        out_shape=[tok_shape, feat_shape, feat_shape, feat_shape, feat_shape],
        compiler_params=_params(1),
        name="inproj",
    )(h, g, wk, wt)


AUX_ROWS = 8
AUX_CHOICE = AUX_ROWS
SCORE_LOOKAHEAD = 4
SUM_ROWS = 16


def _attention_constants():
    start = 2.0 ** (-8.0 / N_HEADS)
    slopes = np.array([start ** (i + 1) for i in range(N_HEADS)], dtype=np.float32)
    pos = np.arange(SEQ)
    local = (pos % BLK).astype(np.float32)
    blk = pos // BLK
    kaux = np.zeros((N_HEADS, SEQ, PAIR_W), np.float32)
    qaux = np.zeros((N_HEADS, AUX_ROWS, SEQ), np.float32)
    for h in range(N_HEADS):
        base = (1 - h % 2) * HEAD_DIM
        kaux[h, :, base + 0] = 1.0
        kaux[h, :, base + 1] = slopes[h] * local
        kaux[h, :, base + 2] = 1.0
        kaux[h, :, base + 3] = slopes[h] * BLK * blk
        kaux[h, pos, base + AUX_CHOICE + blk] = 1.0
        qaux[h, 0] = -slopes[h] * local
        qaux[h, 1] = 1.0
        qaux[h, 2] = -slopes[h] * BLK * blk
        qaux[h, 3] = 1.0
    assert np.array_equal(kaux.astype(BF16).astype(np.float32), kaux)
    assert np.array_equal(qaux.astype(BF16).astype(np.float32), qaux)
    return jnp.asarray(kaux.astype(BF16)), jnp.asarray(qaux)


def _attn_kernel(kaux_ref, qaux_ref, k_ref, qt_ref, vt_ref, o_ref, kaug_ref, choice_ref, vaug_ref):
    lane = lax.broadcasted_iota(jnp.int32, (1, PAIR_W), 1)
    k_all = k_ref[...]
    q_all = qt_ref[0]
    k_mean = jnp.mean(k_all.astype(F32).reshape(N_BLK, BLK, PAIR_W), axis=1)
    blk_of_row = lax.broadcasted_iota(jnp.int32, (N_BLK, SEQ), 0)
    blk_of_query = lax.broadcasted_iota(jnp.int32, (N_BLK, SEQ), 1) // BLK
    in_past = blk_of_row < blk_of_query
    for a in range(2):
        in_head = (lane >= a * HEAD_DIM) & (lane < (a + 1) * HEAD_DIM)
        kaug_ref[a] = jnp.where(in_head, k_all, kaux_ref[a])
        km = jnp.where(in_head, k_mean, 0.0)
        km_hi = km.astype(BF16)
        km_lo = (km - km_hi.astype(F32)).astype(BF16)
        gate = _dot(km_hi, q_all) + _dot(km_lo, q_all)
        gate = jnp.where(in_past, gate, NEG)
        rank = jnp.zeros((N_BLK, SEQ), jnp.int32)
        for other in range(N_BLK):
            row = gate[other:other + 1, :]
            beats = (row > gate) | ((row == gate) & (other < blk_of_row))
            rank = rank + beats.astype(jnp.int32)
        chosen = (rank < TOP_K) & (gate > 0.5 * NEG)
        choice_ref[a] = jnp.where(in_past & jnp.logical_not(chosen), NEG, 0.0)
        vaug_ref[a, :HEAD_DIM, :] = vt_ref[0, a * HEAD_DIM:(a + 1) * HEAD_DIM, :]
        vaug_ref[a, HEAD_DIM:, :] = jnp.ones((SUM_ROWS, SEQ), BF16)

    key_idx = lax.broadcasted_iota(jnp.int32, (BLK, BLK), 0)
    query_idx = lax.broadcasted_iota(jnp.int32, (BLK, BLK), 1)
    causal = key_idx <= query_idx
    aux_fill = jnp.zeros((HEAD_DIM - 2 * AUX_ROWS, BLK), F32)

    def blk_slice(j):
        return slice(j * BLK, (j + 1) * BLK)

    tiles = [(i, a, j) for i in range(N_BLK) for a in range(2) for j in [i] + list(range(i))]
    q_augs, scores, state, outs = {}, {}, {}, {}

    def issue_scores(n):
        i, a, j = tiles[n]
        if (i, a) not in q_augs:
            feat = slice(a * HEAD_DIM, (a + 1) * HEAD_DIM)
            q_head = qt_ref[0, feat, blk_slice(i)].astype(F32) * (HEAD_DIM ** -0.5)
            aux = jnp.concatenate(
                [qaux_ref[a, :, blk_slice(i)], choice_ref[a, :, blk_slice(i)], aux_fill], axis=0)
            q_augs[i, a] = jnp.concatenate(
                [q_head, aux] if a == 0 else [aux, q_head], axis=0).astype(BF16)
        scores[n] = _dot(kaug_ref[a, blk_slice(j), :], q_augs[i, a])

    def consume_scores(n):
        i, a, j = tiles[n]
        s = scores.pop(n)
        if j == i:
            s = jnp.where(causal, s, NEG)
            m = jnp.max(s, axis=0, keepdims=True)
            p = jnp.exp(s - m)
            acc = _dot(vaug_ref[a, :, blk_slice(j)], p.astype(BF16))
        else:
            m_old, acc = state[i, a]
            m = jnp.maximum(m_old, jnp.max(s, axis=0, keepdims=True))
            alpha = jnp.exp(m_old - m)
            p = jnp.exp(s - m)
            acc = alpha * acc + _dot(vaug_ref[a, :, blk_slice(j)], p.astype(BF16))
        state[i, a] = (m, acc)
        if j == (i - 1 if i > 0 else 0):
            outs.setdefault(i, []).append(acc[:HEAD_DIM] / acc[HEAD_DIM:HEAD_DIM + 1])
            del state[i, a]
            if a == 1:
                o_ref[blk_slice(i), :] = jnp.concatenate(outs.pop(i), axis=0).T.astype(BF16)

    for n in range(min(SCORE_LOOKAHEAD, len(tiles))):
        issue_scores(n)
    for n in range(len(tiles)):
        if n + SCORE_LOOKAHEAD < len(tiles):
            issue_scores(n + SCORE_LOOKAHEAD)
        consume_scores(n)


def _attention(kaux, qaux, k, qt, vt):
    tok = pl.BlockSpec((SEQ, PAIR_W), lambda b, p: (b, p))
    feat = pl.BlockSpec((1, PAIR_W, SEQ), lambda b, p: (b, p, 0))
    return pl.pallas_call(
        _attn_kernel,
        grid=(BATCH, ATTN_W // PAIR_W),
        in_specs=[pl.BlockSpec((2, SEQ, PAIR_W), lambda b, p: (p, 0, 0)),
                  pl.BlockSpec((2, AUX_ROWS, SEQ), lambda b, p: (p, 0, 0)),
                  tok, feat, feat],
        out_specs=tok,
        out_shape=jax.ShapeDtypeStruct((TOKENS, ATTN_W), BF16),
        scratch_shapes=[pltpu.VMEM((2, SEQ, PAIR_W), BF16),
                        pltpu.VMEM((2, N_BLK, SEQ), F32),
                        pltpu.VMEM((2, HEAD_DIM + SUM_ROWS, SEQ), BF16)],
        compiler_params=_params(2),
        name="moba_attention",
    )(kaux, qaux, k, qt, vt)


def _mixout_kernel(h_ref, attn_ref, gut_ref, gvt_ref, lng_ref, lnb_ref, ws_ref, bs_ref,
                   an_ref, gn_ref, woa_ref, wog_ref, gpost_ref, o_ref, vn_ref, gated_ref):
    n_chunks = TM // CHUNK
    for c in range(n_chunks):
        cols = slice(c * CHUNK, (c + 1) * CHUNK)
        gv = gvt_ref[0, :, cols].astype(F32).reshape(N_GROUPS, HEAD_DIM, CHUNK)
        mu = jnp.mean(gv, axis=1, keepdims=True)
        var = jnp.mean(jnp.square(gv - mu), axis=1, keepdims=True)
        vn = ((gv - mu) * lax.rsqrt(var + LN_EPS)).reshape(GMLP_W, CHUNK)
        vn = (vn * lng_ref[...] + lnb_ref[...]).astype(BF16).reshape(N_GROUPS, HEAD_DIM, CHUNK)
        vn_ref[:, c * HEAD_DIM:(c + 1) * HEAD_DIM, :] = vn
    t_idx = lax.broadcasted_iota(jnp.int32, (CHUNK, CHUNK), 0)
    s_idx = lax.broadcasted_iota(jnp.int32, (CHUNK, CHUNK), 1)
    ssq = jnp.zeros((n_chunks, 1, CHUNK), F32)
    for g in range(N_GROUPS):
        w = jnp.where(s_idx <= t_idx, ws_ref[g], 0.0).astype(BF16)
        mixed = _dot_nt(vn_ref[g], w) + bs_ref[g:g + 1, :]
        feat = slice(g * HEAD_DIM, (g + 1) * HEAD_DIM)
        gu = jnp.concatenate(
            [gut_ref[0, feat, c * CHUNK:(c + 1) * CHUNK] for c in range(n_chunks)], axis=0)
        gated = (gu.astype(F32) * mixed).reshape(n_chunks, HEAD_DIM, CHUNK)
        ssq = ssq + jnp.sum(gated * gated, axis=1, keepdims=True)
        gated_ref[:, feat, :] = gated
    inv = lax.rsqrt(ssq * (1.0 / GMLP_W) + RMS_EPS)
    attn_n = _rms(attn_ref[...].astype(F32), an_ref[...]).astype(BF16)
    gmlp_n = jnp.concatenate(
        [(gated_ref[c] * inv[c] * gn_ref[...]).T for c in range(n_chunks)], axis=0).astype(BF16)
    m = _dot(attn_n, woa_ref[...]) + _dot(gmlp_n, wog_ref[...])
    o_ref[...] = h_ref[...] + _rms(m, gpost_ref[...])


def _mixout(h, attn, gut, gvt, lng, lnb, ws, bs, an, gn, woa, wog, gpost):
    tiles_per_seq = SEQ // TM
    row = pl.BlockSpec((TM, D_MODEL), lambda i: (i, 0))
    feat = pl.BlockSpec((1, GMLP_W, TM), lambda i: (i // tiles_per_seq, 0, i % tiles_per_seq))
    return pl.pallas_call(
        _mixout_kernel,
        grid=(TOKENS // TM,),
        in_specs=[row, pl.BlockSpec((TM, ATTN_W), lambda i: (i, 0)), feat, feat,
                  _const_spec((GMLP_W, CHUNK)), _const_spec((GMLP_W, CHUNK)),
                  _const_spec((N_GROUPS, CHUNK, CHUNK)), _const_spec((N_GROUPS, CHUNK)),
                  _const_spec((1, ATTN_W)), _const_spec((GMLP_W, CHUNK)),
                  _const_spec((ATTN_W, D_MODEL)), _const_spec((GMLP_W, D_MODEL)),
                  _const_spec((1, D_MODEL))],
        out_specs=row,
        out_shape=jax.ShapeDtypeStruct((TOKENS, D_MODEL), F32),
        scratch_shapes=[pltpu.VMEM((N_GROUPS, (TM // CHUNK) * HEAD_DIM, CHUNK), BF16),
                        pltpu.VMEM((TM // CHUNK, GMLP_W, CHUNK), F32)],
        compiler_params=_params(1),
        name="mixout",
    )(h, attn, gut, gvt, lng, lnb, ws, bs, an, gn, woa, wog, gpost)


def _ple_kernel(h_ref, p_ref, gpre_ref, wg_ref, wp_ref, gpost_ref, o_ref):
    h = h_ref[...]
    hn = _rms(h, gpre_ref[...]).astype(BF16)
    gate = jax.nn.sigmoid(_dot(hn, wg_ref[...]))
    e = gate * _dot(p_ref[...].astype(BF16), wp_ref[...])
    o_ref[...] = h + _rms(e, gpost_ref[...])


def _ple(h, p, gpre, wg, wp, gpost):
    row = pl.BlockSpec((TM, D_MODEL), lambda i: (i, 0))
    return pl.pallas_call(
        _ple_kernel,
        grid=(TOKENS // TM,),
        in_specs=[row, pl.BlockSpec((TM, PLE_DIM), lambda i: (i, 0)),
                  _const_spec((1, D_MODEL)), _const_spec((D_MODEL, D_MODEL)),
                  _const_spec((PLE_DIM, D_MODEL)), _const_spec((1, D_MODEL))],
        out_specs=row,
        out_shape=jax.ShapeDtypeStruct((TOKENS, D_MODEL), F32),
        compiler_params=_params(1),
        name="ple",
    )(h, p, gpre, wg, wp, gpost)


def kernel(x, p, ffn1_pre_norm, ffn1_w_gate, ffn1_w_up, ffn1_w_down, ffn1_post_norm, mix_pre_norm, w_in, gmlp_ln_g, gmlp_ln_b, gmlp_w_s, gmlp_b_s, attn_out_norm, gmlp_out_norm, w_out, mix_post_norm, ffn2_pre_norm, ffn2_w_gate, ffn2_w_up, ffn2_w_down, ffn2_post_norm, ple_pre_norm, ple_w_gate, ple_w_proj, ple_post_norm):
    assert x.shape == (BATCH, SEQ, D_MODEL) and p.shape == (DEPTH, BATCH, SEQ, PLE_DIM)
    kaux, qaux = _attention_constants()
    h = x.reshape(TOKENS, D_MODEL)
    p = p.reshape(DEPTH, TOKENS, PLE_DIM)

    def vec(v):
        return v.reshape(1, -1)

    def col(v):
        return jnp.broadcast_to(v.reshape(-1, 1), (v.shape[0], CHUNK))

    for i in range(DEPTH):
        h = _ffn(h, vec(ffn1_pre_norm[i]), ffn1_w_gate[i].astype(BF16), ffn1_w_up[i].astype(BF16),
                 ffn1_w_down[i].astype(BF16), vec(ffn1_post_norm[i]))
        w_k = w_in[i, :, ATTN_W:2 * ATTN_W].astype(BF16)
        w_t = jnp.concatenate([w_in[i, :, :ATTN_W], w_in[i, :, 2 * ATTN_W:]], axis=1).T.astype(BF16)
        k, qt, vt, gut, gvt = _inproj(h, vec(mix_pre_norm[i]), w_k, w_t)
        attn = _attention(kaux, qaux, k, qt, vt)
        h = _mixout(h, attn, gut, gvt, col(gmlp_ln_g[i]), col(gmlp_ln_b[i]), gmlp_w_s[i],
                    gmlp_b_s[i], vec(attn_out_norm[i]), col(gmlp_out_norm[i]),
                    w_out[i, :ATTN_W].astype(BF16), w_out[i, ATTN_W:].astype(BF16),
                    vec(mix_post_norm[i]))
        h = _ffn(h, vec(ffn2_pre_norm[i]), ffn2_w_gate[i].astype(BF16), ffn2_w_up[i].astype(BF16),
                 ffn2_w_down[i].astype(BF16), vec(ffn2_post_norm[i]))
        h = _ple(h, p[i], vec(ple_pre_norm[i]), ple_w_gate[i].astype(BF16),
                 ple_w_proj[i].astype(BF16), vec(ple_post_norm[i]))
    return h.reshape(BATCH, SEQ, D_MODEL)
```

```python
import numpy as np
import jax
import jax.numpy as jnp
from jax import lax
from jax.experimental import pallas as pl
from jax.experimental.pallas import tpu as pltpu

D_MODEL = 1024
BATCH = 16
SEQ = 2048
DEPTH = 2
HEAD_DIM = 64
N_HEADS = 8
N_GROUPS = 8
ATTN_W = N_HEADS * HEAD_DIM
GMLP_W = N_GROUPS * HEAD_DIM
BLK = 256
N_BLK = SEQ // BLK
TOP_K = 3
CHUNK = 128
D_FF = 2816
PLE_DIM = 256
RMS_EPS = 1e-6
LN_EPS = 1e-5
NEG = -1e30
TOKENS = BATCH * SEQ

PAIR_W = 2 * HEAD_DIM
TM = 512
FF_CHUNK = 256
VMEM_LIMIT = 56 * 1024 * 1024

F32 = jnp.float32
BF16 = jnp.bfloat16


def _rms(x, g):
    return x * lax.rsqrt(jnp.mean(x * x, axis=-1, keepdims=True) + RMS_EPS) * g


def _gelu(x):
    return 0.5 * x * (1.0 + lax.erf(x * np.float32(np.sqrt(0.5))))


def _dot(a, b):
    return jnp.dot(a, b, preferred_element_type=F32)


def _dot_nt(a, b):
    return lax.dot_general(a, b, (((1,), (1,)), ((), ())), preferred_element_type=F32)


def _const_spec(shape):
    nd = len(shape)
    return pl.BlockSpec(shape, lambda *_: (0,) * nd, pipeline_mode=pl.Buffered(1))


def _params(n_axes):
    return pltpu.CompilerParams(
        dimension_semantics=("arbitrary",) * n_axes, vmem_limit_bytes=VMEM_LIMIT)


def _ffn_stage(h, gpre_ref, wg_ref, wu_ref, wd_ref, gpost_ref, a_ref):
    hn = _rms(h, gpre_ref[...]).astype(BF16)
    for c in range(D_FF // FF_CHUNK):
        sl = slice(c * FF_CHUNK, (c + 1) * FF_CHUNK)
        g = _dot(hn, wg_ref[:, sl])
        u = _dot(hn, wu_ref[:, sl])
        a_ref[:, sl] = (g * jax.nn.sigmoid(g) * u).astype(BF16)
    f = _dot(a_ref[...], wd_ref[...])
    return h + 0.5 * _rms(f, gpost_ref[...])


def _inproj_stage(h, g_ref, wk_ref, wt_ref, k_ref, qt_ref, vt_ref, gut_ref, gvt_ref):
    hn = _rms(h, g_ref[...]).astype(BF16)
    k_ref[...] = _dot(hn, wk_ref[...]).astype(BF16)
    for r, (out_ref, act) in enumerate(
            ((qt_ref, None), (vt_ref, None), (gut_ref, _gelu), (gvt_ref, _gelu))):
        z = _dot_nt(wt_ref[r * ATTN_W:(r + 1) * ATTN_W, :], hn)
        out_ref[0] = (z if act is None else act(z)).astype(BF16)


def _mixout_stage(h, attn_ref, gut_ref, gvt_ref, lng_ref, lnb_ref, ws_ref, bs_ref,
                  an_ref, gn_ref, woa_ref, wog_ref, gpost_ref, vn_ref, gated_ref):
    n_chunks = TM // CHUNK
    for c in range(n_chunks):
        cols = slice(c * CHUNK, (c + 1) * CHUNK)
        gv = gvt_ref[0, :, cols].astype(F32).reshape(N_GROUPS, HEAD_DIM, CHUNK)
        mu = jnp.mean(gv, axis=1, keepdims=True)
        var = jnp.mean(jnp.square(gv - mu), axis=1, keepdims=True)
        vn = ((gv - mu) * lax.rsqrt(var + LN_EPS)).reshape(GMLP_W, CHUNK)
        vn = (vn * lng_ref[...] + lnb_ref[...]).astype(BF16).reshape(N_GROUPS, HEAD_DIM, CHUNK)
        vn_ref[:, c * HEAD_DIM:(c + 1) * HEAD_DIM, :] = vn
    t_idx = lax.broadcasted_iota(jnp.int32, (CHUNK, CHUNK), 0)
    s_idx = lax.broadcasted_iota(jnp.int32, (CHUNK, CHUNK), 1)
    ssq = jnp.zeros((n_chunks, 1, CHUNK), F32)
    for g in range(N_GROUPS):
        w = jnp.where(s_idx <= t_idx, ws_ref[g], 0.0).astype(BF16)
        mixed = _dot_nt(vn_ref[g], w) + bs_ref[g:g + 1, :]
        feat = slice(g * HEAD_DIM, (g + 1) * HEAD_DIM)
        gu = jnp.concatenate(
            [gut_ref[0, feat, c * CHUNK:(c + 1) * CHUNK] for c in range(n_chunks)], axis=0)
        gated = (gu.astype(F32) * mixed).reshape(n_chunks, HEAD_DIM, CHUNK)
        ssq = ssq + jnp.sum(gated * gated, axis=1, keepdims=True)
        gated_ref[:, feat, :] = gated
    inv = lax.rsqrt(ssq * (1.0 / GMLP_W) + RMS_EPS)
    attn_n = _rms(attn_ref[...].astype(F32), an_ref[...]).astype(BF16)
    gmlp_n = jnp.concatenate(
        [(gated_ref[c] * inv[c] * gn_ref[...]).T for c in range(n_chunks)], axis=0).astype(BF16)
    m = _dot(attn_n, woa_ref[...]) + _dot(gmlp_n, wog_ref[...])
    return h + _rms(m, gpost_ref[...])


def _ple_stage(h, p_ref, gpre_ref, wg_ref, wp_ref, gpost_ref):
    hn = _rms(h, gpre_ref[...]).astype(BF16)
    gate = jax.nn.sigmoid(_dot(hn, wg_ref[...]))
    e = gate * _dot(p_ref[...].astype(BF16), wp_ref[...])
    return h + _rms(e, gpost_ref[...])


N_FFN_REFS = 5
N_INPROJ_REFS = 3
N_MIX_REFS = 9
N_PLE_REFS = 4


def _split(refs, *sizes):
    groups, at = [], 0
    for n in sizes:
        groups.append(refs[at:at + n])
        at += n
    return (*groups, refs[at:])


def _pre_attention_kernel(h_ref, *refs):
    ffn, inproj, (h1_ref, *inproj_out, a_ref) = _split(refs, N_FFN_REFS, N_INPROJ_REFS)
    h1 = _ffn_stage(h_ref[...], *ffn, a_ref)
    h1_ref[...] = h1
    _inproj_stage(h1, *inproj, *inproj_out)


def _pre_attention(h, ffn, inproj):
    tiles_per_seq = SEQ // TM
    row = pl.BlockSpec((TM, D_MODEL), lambda i: (i, 0))
    tok = pl.BlockSpec((TM, ATTN_W), lambda i: (i, 0))
    feat = pl.BlockSpec((1, ATTN_W, TM), lambda i: (i // tiles_per_seq, 0, i % tiles_per_seq))
    tok_shape = jax.ShapeDtypeStruct((TOKENS, ATTN_W), BF16)
    feat_shape = jax.ShapeDtypeStruct((BATCH, ATTN_W, SEQ), BF16)
    assert len(ffn) == N_FFN_REFS and len(inproj) == N_INPROJ_REFS
    operands = (*ffn, *inproj)
    return pl.pallas_call(
        _pre_attention_kernel,
        grid=(TOKENS // TM,),
        in_specs=[row] + [_const_spec(w.shape) for w in operands],
        out_specs=[row, tok, feat, feat, feat, feat],
        out_shape=[jax.ShapeDtypeStruct((TOKENS, D_MODEL), F32), tok_shape,
                   feat_shape, feat_shape, feat_shape, feat_shape],
        scratch_shapes=[pltpu.VMEM((TM, D_FF), BF16)],
        compiler_params=_params(1),
        name="ffn1_inproj",
    )(h, *operands)


def _post_attention_kernel(h_ref, attn_ref, gut_ref, gvt_ref, p_ref, *refs):
    mix, ffn, ple, (o_ref, vn_ref, gated_ref, a_ref) = _split(refs, N_MIX_REFS, N_FFN_REFS, N_PLE_REFS)
    h = _mixout_stage(h_ref[...], attn_ref, gut_ref, gvt_ref, *mix, vn_ref, gated_ref)
    h = _ffn_stage(h, *ffn, a_ref)
    o_ref[...] = _ple_stage(h, p_ref, *ple)


def _post_attention(h, attn, gut, gvt, p, mix, ffn, ple):
    tiles_per_seq = SEQ // TM
    row = pl.BlockSpec((TM, D_MODEL), lambda i: (i, 0))
    feat = pl.BlockSpec((1, GMLP_W, TM), lambda i: (i // tiles_per_seq, 0, i % tiles_per_seq))
    assert len(mix) == N_MIX_REFS and len(ffn) == N_FFN_REFS and len(ple) == N_PLE_REFS
    operands = (*mix, *ffn, *ple)
    return pl.pallas_call(
        _post_attention_kernel,
        grid=(TOKENS // TM,),
        in_specs=[row, pl.BlockSpec((TM, ATTN_W), lambda i: (i, 0)), feat, feat,
                  pl.BlockSpec((TM, PLE_DIM), lambda i: (i, 0))]
                 + [_const_spec(w.shape) for w in operands],
        out_specs=row,
        out_shape=jax.ShapeDtypeStruct((TOKENS, D_MODEL), F32),
        scratch_shapes=[pltpu.VMEM((N_GROUPS, (TM // CHUNK) * HEAD_DIM, CHUNK), BF16),
                        pltpu.VMEM((TM // CHUNK, GMLP_W, CHUNK), F32),
                        pltpu.VMEM((TM, D_FF), BF16)],
        compiler_params=_params(1),
        name="mixout_ffn2_ple",
    )(h, attn, gut, gvt, p, *operands)


SPLIT = 4
POS_ROWS = 4 * SPLIT
CHOICE_ROWS = N_BLK
AUX_CHOICE = POS_ROWS
SCORE_LOOKAHEAD = 4
SUM_ROWS = 16
LOG2E = float(np.log2(np.e))


def _split_bf16(x):
    terms, rest = [], x
    for _ in range(SPLIT):
        t = rest.astype(BF16).astype(np.float64)
        terms.append(t)
        rest = rest - t
    assert np.max(np.abs(rest)) <= 1e-7 * max(1.0, np.max(np.abs(x)))
    return terms


def _attention_constants():
    start = 2.0 ** (-8.0 / N_HEADS)
    slopes = np.array([start ** (i + 1) for i in range(N_HEADS)], dtype=np.float32).astype(np.float64)
    pos = np.arange(SEQ)
    local = (pos % BLK).astype(np.float64)
    blk = pos // BLK
    kaux = np.zeros((N_HEADS, SEQ, PAIR_W), np.float64)
    qaux = np.zeros((N_HEADS, POS_ROWS, SEQ), np.float64)
    for h in range(N_HEADS):
        base = (1 - h % 2) * HEAD_DIM
        c = LOG2E * slopes[h]
        fine, coarse = _split_bf16(c * local), _split_bf16(c * BLK * blk)
        for n in range(SPLIT):
            kaux[h, :, base + n] = 1.0
            qaux[h, n] = -fine[n]
            kaux[h, :, base + SPLIT + n] = fine[n]
            qaux[h, SPLIT + n] = 1.0
            kaux[h, :, base + 2 * SPLIT + n] = 1.0
            qaux[h, 2 * SPLIT + n] = -coarse[n]
            kaux[h, :, base + 3 * SPLIT + n] = coarse[n]
            qaux[h, 3 * SPLIT + n] = 1.0
        kaux[h, pos, base + AUX_CHOICE + blk] = 1.0
    assert np.array_equal(kaux.astype(BF16).astype(np.float64), kaux)
    assert np.array_equal(qaux.astype(BF16).astype(np.float64), qaux)
    return jnp.asarray(kaux.astype(BF16)), jnp.asarray(qaux.astype(np.float32))


def _attn_kernel(kaux_ref, qaux_ref, k_ref, qt_ref, vt_ref, o_ref, kaug_ref, choice_ref, vaug_ref):
    lane = lax.broadcasted_iota(jnp.int32, (1, PAIR_W), 1)
    k_all = k_ref[...]
    q_all = qt_ref[0]
    k_mean = jnp.mean(k_all.astype(F32).reshape(N_BLK, BLK, PAIR_W), axis=1)
    blk_of_row = lax.broadcasted_iota(jnp.int32, (N_BLK, SEQ), 0)
    blk_of_query = lax.broadcasted_iota(jnp.int32, (N_BLK, SEQ), 1) // BLK
    in_past = blk_of_row < blk_of_query
    for a in range(2):
        in_head = (lane >= a * HEAD_DIM) & (lane < (a + 1) * HEAD_DIM)
        kaug_ref[a] = jnp.where(in_head, k_all, kaux_ref[a])
        km = jnp.where(in_head, k_mean, 0.0)
        km_hi = km.astype(BF16)
        km_lo = (km - km_hi.astype(F32)).astype(BF16)
        gate = _dot(km_hi, q_all) + _dot(km_lo, q_all)
        gate = jnp.where(in_past, gate, NEG)
        rank = jnp.zeros((N_BLK, SEQ), jnp.int32)
        for other in range(N_BLK):
            row = gate[other:other + 1, :]
            beats = (row > gate) | ((row == gate) & (other < blk_of_row))
            rank = rank + beats.astype(jnp.int32)
        chosen = (rank < TOP_K) & (gate > 0.5 * NEG)
        choice_ref[a] = jnp.where(in_past & jnp.logical_not(chosen), NEG, 0.0)
        vaug_ref[a, :HEAD_DIM, :] = vt_ref[0, a * HEAD_DIM:(a + 1) * HEAD_DIM, :]
        vaug_ref[a, HEAD_DIM:, :] = jnp.ones((SUM_ROWS, SEQ), BF16)

    key_idx = lax.broadcasted_iota(jnp.int32, (BLK, BLK), 0)
    query_idx = lax.broadcasted_iota(jnp.int32, (BLK, BLK), 1)
    causal = key_idx <= query_idx
    aux_fill = jnp.zeros((HEAD_DIM - POS_ROWS - CHOICE_ROWS, BLK), F32)

    def blk_slice(j):
        return slice(j * BLK, (j + 1) * BLK)

    tiles = [(i, a, j) for i in range(N_BLK) for a in range(2) for j in [i] + list(range(i))]
    q_augs, scores, state, outs = {}, {}, {}, {}

    def issue_scores(n):
        i, a, j = tiles[n]
        if (i, a) not in q_augs:
            feat = slice(a * HEAD_DIM, (a + 1) * HEAD_DIM)
            q_head = qt_ref[0, feat, blk_slice(i)].astype(F32) * (LOG2E * HEAD_DIM ** -0.5)
            aux = jnp.concatenate(
                [qaux_ref[a, :, blk_slice(i)], choice_ref[a, :, blk_slice(i)], aux_fill], axis=0)
            q_augs[i, a] = jnp.concatenate(
                [q_head, aux] if a == 0 else [aux, q_head], axis=0).astype(BF16)
        scores[n] = _dot(kaug_ref[a, blk_slice(j), :], q_augs[i, a])

    def consume_scores(n):
        i, a, j = tiles[n]
        s = scores.pop(n)
        if j == i:
            s = jnp.where(causal, s, NEG)
            m = jnp.max(s, axis=0, keepdims=True)
            p = jnp.exp2(s - m)
            acc = _dot(vaug_ref[a, :, blk_slice(j)], p.astype(BF16))
        else:
            m_old, acc = state[i, a]
            m = jnp.maximum(m_old, jnp.max(s, axis=0, keepdims=True))
            alpha = jnp.exp2(m_old - m)
            p = jnp.exp2(s - m)
            acc = alpha * acc + _dot(vaug_ref[a, :, blk_slice(j)], p.astype(BF16))
        state[i, a] = (m, acc)
        if j == (i - 1 if i > 0 else 0):
            outs.setdefault(i, []).append(acc[:HEAD_DIM] / acc[HEAD_DIM:HEAD_DIM + 1])
            del state[i, a]
            if a == 1:
                o_ref[blk_slice(i), :] = jnp.concatenate(outs.pop(i), axis=0).T.astype(BF16)

    for n in range(min(SCORE_LOOKAHEAD, len(tiles))):
        issue_scores(n)
    for n in range(len(tiles)):
        if n + SCORE_LOOKAHEAD < len(tiles):
            issue_scores(n + SCORE_LOOKAHEAD)
        consume_scores(n)


def _attention(kaux, qaux, k, qt, vt):
    tok = pl.BlockSpec((SEQ, PAIR_W), lambda b, p: (b, p))
    feat = pl.BlockSpec((1, PAIR_W, SEQ), lambda b, p: (b, p, 0))
    return pl.pallas_call(
        _attn_kernel,
        grid=(BATCH, ATTN_W // PAIR_W),
        in_specs=[pl.BlockSpec((2, SEQ, PAIR_W), lambda b, p: (p, 0, 0)),
                  pl.BlockSpec((2, POS_ROWS, SEQ), lambda b, p: (p, 0, 0)),
                  tok, feat, feat],
        out_specs=tok,
        out_shape=jax.ShapeDtypeStruct((TOKENS, ATTN_W), BF16),
        scratch_shapes=[pltpu.VMEM((2, SEQ, PAIR_W), BF16),
                        pltpu.VMEM((2, N_BLK, SEQ), F32),
                        pltpu.VMEM((2, HEAD_DIM + SUM_ROWS, SEQ), BF16)],
        compiler_params=_params(2),
        name="moba_attention",
    )(kaux, qaux, k, qt, vt)


def kernel(x, p, ffn1_pre_norm, ffn1_w_gate, ffn1_w_up, ffn1_w_down, ffn1_post_norm, mix_pre_norm, w_in, gmlp_ln_g, gmlp_ln_b, gmlp_w_s, gmlp_b_s, attn_out_norm, gmlp_out_norm, w_out, mix_post_norm, ffn2_pre_norm, ffn2_w_gate, ffn2_w_up, ffn2_w_down, ffn2_post_norm, ple_pre_norm, ple_w_gate, ple_w_proj, ple_post_norm):
    assert x.shape == (BATCH, SEQ, D_MODEL) and p.shape == (DEPTH, BATCH, SEQ, PLE_DIM)
    kaux, qaux = _attention_constants()
    h = x.reshape(TOKENS, D_MODEL)
    p = p.reshape(DEPTH, TOKENS, PLE_DIM)

    def vec(v):
        return v.reshape(1, -1)

    def col(v):
        return jnp.broadcast_to(v.reshape(-1, 1), (v.shape[0], CHUNK))

    def bf(w):
        return w.astype(BF16)

    for i in range(DEPTH):
        ffn1 = (vec(ffn1_pre_norm[i]), bf(ffn1_w_gate[i]), bf(ffn1_w_up[i]), bf(ffn1_w_down[i]),
                vec(ffn1_post_norm[i]))
        w_k = bf(w_in[i, :, ATTN_W:2 * ATTN_W])
        w_t = bf(jnp.concatenate([w_in[i, :, :ATTN_W], w_in[i, :, 2 * ATTN_W:]], axis=1).T)
        h, k, qt, vt, gut, gvt = _pre_attention(h, ffn1, (vec(mix_pre_norm[i]), w_k, w_t))
        attn = _attention(kaux, qaux, k, qt, vt)
        mix = (col(gmlp_ln_g[i]), col(gmlp_ln_b[i]), gmlp_w_s[i], gmlp_b_s[i],
               vec(attn_out_norm[i]), col(gmlp_out_norm[i]),
               bf(w_out[i, :ATTN_W]), bf(w_out[i, ATTN_W:]), vec(mix_post_norm[i]))
        ffn2 = (vec(ffn2_pre_norm[i]), bf(ffn2_w_gate[i]), bf(ffn2_w_up[i]), bf(ffn2_w_down[i]),
                vec(ffn2_post_norm[i]))
        ple = (vec(ple_pre_norm[i]), bf(ple_w_gate[i]), bf(ple_w_proj[i]), vec(ple_post_norm[i]))
        h = _post_attention(h, attn, gut, gvt, p[i], mix, ffn2, ple)
    return h.reshape(BATCH, SEQ, D_MODEL)
```

```python
import numpy as np
import jax
import jax.numpy as jnp
from jax import lax
from jax.experimental import pallas as pl
from jax.experimental.pallas import tpu as pltpu

D_MODEL = 1024
BATCH = 16
SEQ = 2048
DEPTH = 2
HEAD_DIM = 64
N_HEADS = 8
N_GROUPS = 8
ATTN_W = N_HEADS * HEAD_DIM
GMLP_W = N_GROUPS * HEAD_DIM
BLK = 256
N_BLK = SEQ // BLK
TOP_K = 3
CHUNK = 128
D_FF = 2816
PLE_DIM = 256
RMS_EPS = 1e-6
LN_EPS = 1e-5
NEG = -1e30
TOKENS = BATCH * SEQ

PAIR_W = 2 * HEAD_DIM
TM = 512
SUB = 256
FF_CHUNK = 256
VMEM_LIMIT = 56 * 1024 * 1024

F32 = jnp.float32
BF16 = jnp.bfloat16


def _rms(x, g):
    return x * lax.rsqrt(jnp.mean(x * x, axis=-1, keepdims=True) + RMS_EPS) * g


def _gelu(x):
    return 0.5 * x * (1.0 + lax.erf(x * np.float32(np.sqrt(0.5))))


def _dot(a, b):
    return jnp.dot(a, b, preferred_element_type=F32)


def _dot_nt(a, b):
    return lax.dot_general(a, b, (((1,), (1,)), ((), ())), preferred_element_type=F32)


def _layer_operand(stacked, layer, block=None, index=None):
    block = stacked.shape[1:] if block is None else block
    index = (0,) * len(block) if index is None else index
    spec = pl.BlockSpec((None, *block), lambda *_: (layer, *index), pipeline_mode=pl.Buffered(1))
    return stacked, spec


def _params(n_axes):
    return pltpu.CompilerParams(
        dimension_semantics=("arbitrary",) * n_axes, vmem_limit_bytes=VMEM_LIMIT)


def _ffn_stage(h, rows, gpre_ref, wg_ref, wu_ref, wd_ref, gpost_ref, a_ref):
    hn = _rms(h, gpre_ref[...]).astype(BF16)
    for c in range(D_FF // FF_CHUNK):
        sl = slice(c * FF_CHUNK, (c + 1) * FF_CHUNK)
        g = _dot(hn, wg_ref[:, sl])
        u = _dot(hn, wu_ref[:, sl])
        a_ref[rows, sl] = (g * jax.nn.sigmoid(g) * u).astype(BF16)
    f = _dot(a_ref[rows, :], wd_ref[...])
    return h + 0.5 * _rms(f, gpost_ref[...])


def _inproj_stage(h, rows, g_ref, wk_ref, wt_ref, k_ref, qt_ref, vt_ref, gut_ref, gvt_ref):
    hn = _rms(h, g_ref[...]).astype(BF16)
    k_ref[rows, :] = _dot(hn, wk_ref[...]).astype(BF16)
    for r, (out_ref, act) in enumerate(
            ((qt_ref, None), (vt_ref, None), (gut_ref, _gelu), (gvt_ref, _gelu))):
        z = _dot_nt(wt_ref[r * ATTN_W:(r + 1) * ATTN_W, :], hn)
        out_ref[0, :, rows] = (z if act is None else act(z)).astype(BF16)


def _mixout_stage(h, rows, attn_ref, gut_ref, gvt_ref, lng_ref, lnb_ref, ws_ref, bs_ref,
                  an_ref, gn_ref, woa_ref, wog_ref, gpost_ref, vn_ref, gated_ref):
    n_chunks = SUB // CHUNK
    first = rows.start // CHUNK

    def chunk_cols(c):
        return slice((first + c) * CHUNK, (first + c + 1) * CHUNK)

    for c in range(n_chunks):
        gv = gvt_ref[0, :, chunk_cols(c)].astype(F32).reshape(N_GROUPS, HEAD_DIM, CHUNK)
        mu = jnp.mean(gv, axis=1, keepdims=True)
        var = jnp.mean(jnp.square(gv - mu), axis=1, keepdims=True)
        vn = ((gv - mu) * lax.rsqrt(var + LN_EPS)).reshape(GMLP_W, CHUNK)
        vn = (vn * lng_ref[...] + lnb_ref[...]).astype(BF16).reshape(N_GROUPS, HEAD_DIM, CHUNK)
        vn_ref[:, c * HEAD_DIM:(c + 1) * HEAD_DIM, :] = vn
    t_idx = lax.broadcasted_iota(jnp.int32, (CHUNK, CHUNK), 0)
    s_idx = lax.broadcasted_iota(jnp.int32, (CHUNK, CHUNK), 1)
    ssq = jnp.zeros((n_chunks, 1, CHUNK), F32)
    for g in range(N_GROUPS):
        w = jnp.where(s_idx <= t_idx, ws_ref[g], 0.0).astype(BF16)
        mixed = _dot_nt(vn_ref[g], w) + bs_ref[g:g + 1, :]
        feat = slice(g * HEAD_DIM, (g + 1) * HEAD_DIM)
        gu = jnp.concatenate([gut_ref[0, feat, chunk_cols(c)] for c in range(n_chunks)], axis=0)
        gated = (gu.astype(F32) * mixed).reshape(n_chunks, HEAD_DIM, CHUNK)
        ssq = ssq + jnp.sum(gated * gated, axis=1, keepdims=True)
        gated_ref[:, feat, :] = gated
    inv = lax.rsqrt(ssq * (1.0 / GMLP_W) + RMS_EPS)
    attn_n = _rms(attn_ref[rows, :].astype(F32), an_ref[...]).astype(BF16)
    gmlp_n = jnp.concatenate(
        [(gated_ref[c] * inv[c] * gn_ref[...]).T for c in range(n_chunks)], axis=0).astype(BF16)
    m = _dot(attn_n, woa_ref[...]) + _dot(gmlp_n, wog_ref[...])
    return h + _rms(m, gpost_ref[...])


def _ple_stage(h, rows, p_ref, gpre_ref, wg_ref, wp_ref, gpost_ref):
    hn = _rms(h, gpre_ref[...]).astype(BF16)
    gate = jax.nn.sigmoid(_dot(hn, wg_ref[...]))
    e = gate * _dot(p_ref[rows, :].astype(BF16), wp_ref[...])
    return h + _rms(e, gpost_ref[...])


N_FFN_REFS = 5
N_INPROJ_REFS = 3
N_MIX_REFS = 9
N_PLE_REFS = 4
SUB_TILES = [slice(r * SUB, (r + 1) * SUB) for r in range(TM // SUB)]


def _split(refs, *sizes):
    groups, at = [], 0
    for n in sizes:
        groups.append(refs[at:at + n])
        at += n
    return (*groups, refs[at:])


def _pre_attention_kernel(h_ref, *refs):
    ffn, inproj, (h1_ref, *inproj_out, a_ref) = _split(refs, N_FFN_REFS, N_INPROJ_REFS)
    hs = [_ffn_stage(h_ref[rows, :], rows, *ffn, a_ref) for rows in SUB_TILES]
    for rows, h1 in zip(SUB_TILES, hs):
        h1_ref[rows, :] = h1
        _inproj_stage(h1, rows, *inproj, *inproj_out)


def _pre_attention(h, ffn, inproj):
    tiles_per_seq = SEQ // TM
    row = pl.BlockSpec((TM, D_MODEL), lambda i: (i, 0))
    tok = pl.BlockSpec((TM, ATTN_W), lambda i: (i, 0))
    feat = pl.BlockSpec((1, ATTN_W, TM), lambda i: (i // tiles_per_seq, 0, i % tiles_per_seq))
    tok_shape = jax.ShapeDtypeStruct((TOKENS, ATTN_W), BF16)
    feat_shape = jax.ShapeDtypeStruct((BATCH, ATTN_W, SEQ), BF16)
    assert len(ffn) == N_FFN_REFS and len(inproj) == N_INPROJ_REFS
    operands, specs = zip(*ffn, *inproj)
    return pl.pallas_call(
        _pre_attention_kernel,
        grid=(TOKENS // TM,),
        in_specs=[row, *specs],
        out_specs=[row, tok, feat, feat, feat, feat],
        out_shape=[jax.ShapeDtypeStruct((TOKENS, D_MODEL), F32), tok_shape,
                   feat_shape, feat_shape, feat_shape, feat_shape],
        scratch_shapes=[pltpu.VMEM((TM, D_FF), BF16)],
        compiler_params=_params(1),
        name="ffn1_inproj",
    )(h, *operands)


def _post_attention_kernel(h_ref, attn_ref, gut_ref, gvt_ref, p_ref, *refs):
    mix, ffn, ple, (o_ref, vn_ref, gated_ref, a_ref) = _split(refs, N_MIX_REFS, N_FFN_REFS, N_PLE_REFS)
    hs = [_mixout_stage(h_ref[rows, :], rows, attn_ref, gut_ref, gvt_ref, *mix,
                        vn_ref.at[r], gated_ref.at[r]) for r, rows in enumerate(SUB_TILES)]
    hs = [_ffn_stage(h, rows, *ffn, a_ref) for rows, h in zip(SUB_TILES, hs)]
    for rows, h in zip(SUB_TILES, hs):
        o_ref[rows, :] = _ple_stage(h, rows, p_ref, *ple)


def _post_attention(h, attn, gut, gvt, p, layer, mix, ffn, ple):
    tiles_per_seq = SEQ // TM
    row = pl.BlockSpec((TM, D_MODEL), lambda i: (i, 0))
    feat = pl.BlockSpec((1, GMLP_W, TM), lambda i: (i // tiles_per_seq, 0, i % tiles_per_seq))
    assert len(mix) == N_MIX_REFS and len(ffn) == N_FFN_REFS and len(ple) == N_PLE_REFS
    operands, specs = zip(*mix, *ffn, *ple)
    return pl.pallas_call(
        _post_attention_kernel,
        grid=(TOKENS // TM,),
        in_specs=[row, pl.BlockSpec((TM, ATTN_W), lambda i: (i, 0)), feat, feat,
                  pl.BlockSpec((None, TM, PLE_DIM), lambda i: (layer, i, 0)), *specs],
        out_specs=row,
        out_shape=jax.ShapeDtypeStruct((TOKENS, D_MODEL), F32),
        scratch_shapes=[pltpu.VMEM((TM // SUB, N_GROUPS, (SUB // CHUNK) * HEAD_DIM, CHUNK), BF16),
                        pltpu.VMEM((TM // SUB, SUB // CHUNK, GMLP_W, CHUNK), F32),
                        pltpu.VMEM((TM, D_FF), BF16)],
        compiler_params=_params(1),
        name="mixout_ffn2_ple",
    )(h, attn, gut, gvt, p, *operands)


SPLIT = 4
POS_ROWS = 4 * SPLIT
CHOICE_ROWS = N_BLK
AUX_CHOICE = POS_ROWS
SCORE_LOOKAHEAD = 4
SUM_ROWS = 16
LOG2E = float(np.log2(np.e))


def _split_bf16(x):
    terms, rest = [], x
    for _ in range(SPLIT):
        t = rest.astype(BF16).astype(np.float64)
        terms.append(t)
        rest = rest - t
    assert np.max(np.abs(rest)) <= 1e-7 * max(1.0, np.max(np.abs(x)))
    return terms


def _attention_constants():
    start = 2.0 ** (-8.0 / N_HEADS)
    slopes = np.array([start ** (i + 1) for i in range(N_HEADS)], dtype=np.float32).astype(np.float64)
    pos = np.arange(SEQ)
    local = (pos % BLK).astype(np.float64)
    blk = pos // BLK
    kaux = np.zeros((N_HEADS, SEQ, PAIR_W), np.float64)
    qaux = np.zeros((N_HEADS, POS_ROWS, SEQ), np.float64)
    for h in range(N_HEADS):
        base = (1 - h % 2) * HEAD_DIM
        c = LOG2E * slopes[h]
        fine, coarse = _split_bf16(c * local), _split_bf16(c * BLK * blk)
        for n in range(SPLIT):
            kaux[h, :, base + n] = 1.0
            qaux[h, n] = -fine[n]
            kaux[h, :, base + SPLIT + n] = fine[n]
            qaux[h, SPLIT + n] = 1.0
            kaux[h, :, base + 2 * SPLIT + n] = 1.0
            qaux[h, 2 * SPLIT + n] = -coarse[n]
            kaux[h, :, base + 3 * SPLIT + n] = coarse[n]
            qaux[h, 3 * SPLIT + n] = 1.0
        kaux[h, pos, base + AUX_CHOICE + blk] = 1.0
    assert np.array_equal(kaux.astype(BF16).astype(np.float64), kaux)
    assert np.array_equal(qaux.astype(BF16).astype(np.float64), qaux)
    return jnp.asarray(kaux.astype(BF16)), jnp.asarray(qaux.astype(np.float32))


def _attn_kernel(kaux_ref, qaux_ref, k_ref, qt_ref, vt_ref, o_ref, kaug_ref, choice_ref, vaug_ref):
    lane = lax.broadcasted_iota(jnp.int32, (1, PAIR_W), 1)
    k_all = k_ref[...]
    q_all = qt_ref[0]
    k_mean = jnp.mean(k_all.astype(F32).reshape(N_BLK, BLK, PAIR_W), axis=1)
    blk_of_row = lax.broadcasted_iota(jnp.int32, (N_BLK, SEQ), 0)
    blk_of_query = lax.broadcasted_iota(jnp.int32, (N_BLK, SEQ), 1) // BLK
    in_past = blk_of_row < blk_of_query
    for a in range(2):
        in_head = (lane >= a * HEAD_DIM) & (lane < (a + 1) * HEAD_DIM)
        kaug_ref[a] = jnp.where(in_head, k_all, kaux_ref[a])
        km = jnp.where(in_head, k_mean, 0.0)
        km_hi = km.astype(BF16)
        km_lo = (km - km_hi.astype(F32)).astype(BF16)
        gate = _dot(km_hi, q_all) + _dot(km_lo, q_all)
        gate = jnp.where(in_past, gate, NEG)
        rank = jnp.zeros((N_BLK, SEQ), jnp.int32)
        for other in range(N_BLK):
            row = gate[other:other + 1, :]
            beats = (row > gate) | ((row == gate) & (other < blk_of_row))
            rank = rank + beats.astype(jnp.int32)
        chosen = (rank < TOP_K) & (gate > 0.5 * NEG)
        choice_ref[a] = jnp.where(in_past & jnp.logical_not(chosen), NEG, 0.0)
        vaug_ref[a, :HEAD_DIM, :] = vt_ref[0, a * HEAD_DIM:(a + 1) * HEAD_DIM, :]
        vaug_ref[a, HEAD_DIM:, :] = jnp.ones((SUM_ROWS, SEQ), BF16)

    key_idx = lax.broadcasted_iota(jnp.int32, (BLK, BLK), 0)
    query_idx = lax.broadcasted_iota(jnp.int32, (BLK, BLK), 1)
    causal = key_idx <= query_idx
    aux_fill = jnp.zeros((HEAD_DIM - POS_ROWS - CHOICE_ROWS, BLK), F32)

    def blk_slice(j):
        return slice(j * BLK, (j + 1) * BLK)

    tiles = [(i, a, j) for i in range(N_BLK) for a in range(2) for j in [i] + list(range(i))]
    q_augs, scores, state, outs = {}, {}, {}, {}

    def issue_scores(n):
        i, a, j = tiles[n]
        if (i, a) not in q_augs:
            feat = slice(a * HEAD_DIM, (a + 1) * HEAD_DIM)
            q_head = qt_ref[0, feat, blk_slice(i)].astype(F32) * (LOG2E * HEAD_DIM ** -0.5)
            aux = jnp.concatenate(
                [qaux_ref[a, :, blk_slice(i)], choice_ref[a, :, blk_slice(i)], aux_fill], axis=0)
            q_augs[i, a] = jnp.concatenate(
                [q_head, aux] if a == 0 else [aux, q_head], axis=0).astype(BF16)
        scores[n] = _dot(kaug_ref[a, blk_slice(j), :], q_augs[i, a])

    def consume_scores(n):
        i, a, j = tiles[n]
        s = scores.pop(n)
        if j == i:
            s = jnp.where(causal, s, NEG)
            m = jnp.max(s, axis=0, keepdims=True)
            p = jnp.exp2(s - m)
            acc = _dot(vaug_ref[a, :, blk_slice(j)], p.astype(BF16))
        else:
            m_old, acc = state[i, a]
            m = jnp.maximum(m_old, jnp.max(s, axis=0, keepdims=True))
            alpha = jnp.exp2(m_old - m)
            p = jnp.exp2(s - m)
            acc = alpha * acc + _dot(vaug_ref[a, :, blk_slice(j)], p.astype(BF16))
        state[i, a] = (m, acc)
        if j == (i - 1 if i > 0 else 0):
            outs.setdefault(i, []).append(acc[:HEAD_DIM] / acc[HEAD_DIM:HEAD_DIM + 1])
            del state[i, a]
            if a == 1:
                o_ref[blk_slice(i), :] = jnp.concatenate(outs.pop(i), axis=0).T.astype(BF16)

    for n in range(min(SCORE_LOOKAHEAD, len(tiles))):
        issue_scores(n)
    for n in range(len(tiles)):
        if n + SCORE_LOOKAHEAD < len(tiles):
            issue_scores(n + SCORE_LOOKAHEAD)
        consume_scores(n)


def _attention(kaux, qaux, k, qt, vt):
    tok = pl.BlockSpec((SEQ, PAIR_W), lambda b, p: (b, p))
    feat = pl.BlockSpec((1, PAIR_W, SEQ), lambda b, p: (b, p, 0))
    return pl.pallas_call(
        _attn_kernel,
        grid=(BATCH, ATTN_W // PAIR_W),
        in_specs=[pl.BlockSpec((2, SEQ, PAIR_W), lambda b, p: (p, 0, 0)),
                  pl.BlockSpec((2, POS_ROWS, SEQ), lambda b, p: (p, 0, 0)),
                  tok, feat, feat],
        out_specs=tok,
        out_shape=jax.ShapeDtypeStruct((TOKENS, ATTN_W), BF16),
        scratch_shapes=[pltpu.VMEM((2, SEQ, PAIR_W), BF16),
                        pltpu.VMEM((2, N_BLK, SEQ), F32),
                        pltpu.VMEM((2, HEAD_DIM + SUM_ROWS, SEQ), BF16)],
        compiler_params=_params(2),
        name="moba_attention",
    )(kaux, qaux, k, qt, vt)


CAST_ROWS = 256


def _cast_kernel(*refs):
    n = len(refs) // 2
    for src, dst in zip(refs[:n], refs[n:]):
        dst[...] = src[...].astype(BF16)


def _cast_bf16(*stacked):
    depth, rows, cols = stacked[0].shape
    assert all(w.shape == stacked[0].shape for w in stacked) and rows % CAST_ROWS == 0
    spec = pl.BlockSpec((None, CAST_ROWS, cols), lambda d, r: (d, r, 0))
    return pl.pallas_call(
        _cast_kernel,
        grid=(depth, rows // CAST_ROWS),
        in_specs=[spec] * len(stacked),
        out_specs=[spec] * len(stacked),
        out_shape=[jax.ShapeDtypeStruct(w.shape, BF16) for w in stacked],
        compiler_params=_params(2),
        name="cast_bf16",
    )(*stacked)


def _transpose_cast_kernel(w_ref, o_ref):
    o_ref[...] = w_ref[...].T.astype(BF16)


def _inproj_weights_transposed(w_in):
    return pl.pallas_call(
        _transpose_cast_kernel,
        grid=(DEPTH, 4),
        in_specs=[pl.BlockSpec((None, D_MODEL, ATTN_W), lambda d, r: (d, 0, r + jnp.minimum(r, 1)))],
        out_specs=pl.BlockSpec((None, ATTN_W, D_MODEL), lambda d, r: (d, r, 0)),
        out_shape=jax.ShapeDtypeStruct((DEPTH, 4 * ATTN_W, D_MODEL), BF16),
        compiler_params=_params(2),
        name="inproj_weights_t",
    )(w_in)


def kernel(x, p, ffn1_pre_norm, ffn1_w_gate, ffn1_w_up, ffn1_w_down, ffn1_post_norm, mix_pre_norm, w_in, gmlp_ln_g, gmlp_ln_b, gmlp_w_s, gmlp_b_s, attn_out_norm, gmlp_out_norm, w_out, mix_post_norm, ffn2_pre_norm, ffn2_w_gate, ffn2_w_up, ffn2_w_down, ffn2_post_norm, ple_pre_norm, ple_w_gate, ple_w_proj, ple_post_norm):
    assert x.shape == (BATCH, SEQ, D_MODEL) and p.shape == (DEPTH, BATCH, SEQ, PLE_DIM)
    assert ATTN_W == GMLP_W and w_in.shape == (DEPTH, D_MODEL, 5 * ATTN_W)
    kaux, qaux = _attention_constants()
    h = x.reshape(TOKENS, D_MODEL)
    p = p.reshape(DEPTH, TOKENS, PLE_DIM)

    def vec(v):
        return v.reshape(DEPTH, 1, -1)

    def col(v):
        return jnp.broadcast_to(v[:, :, None], (*v.shape, CHUNK))

    f1g, f1u, f2g, f2u = _cast_bf16(ffn1_w_gate, ffn1_w_up, ffn2_w_gate, ffn2_w_up)
    f1d, f2d = _cast_bf16(ffn1_w_down, ffn2_w_down)
    wo, pwg = _cast_bf16(w_out, ple_w_gate)
    w_t = _inproj_weights_transposed(w_in)
    w_k = w_in[:, :, ATTN_W:2 * ATTN_W].astype(BF16)
    pwp = ple_w_proj.astype(BF16)
    ln_g, ln_b, gn = col(gmlp_ln_g), col(gmlp_ln_b), col(gmlp_out_norm)

    for i in range(DEPTH):
        def layer(w, block=None, index=None):
            return _layer_operand(w, i, block, index)

        ffn1 = (layer(vec(ffn1_pre_norm)), layer(f1g), layer(f1u), layer(f1d), layer(vec(ffn1_post_norm)))
        inproj = (layer(vec(mix_pre_norm)), layer(w_k), layer(w_t))
        h, k, qt, vt, gut, gvt = _pre_attention(h, ffn1, inproj)
        attn = _attention(kaux, qaux, k, qt, vt)
        mix = (layer(ln_g), layer(ln_b), layer(gmlp_w_s), layer(gmlp_b_s),
               layer(vec(attn_out_norm)), layer(gn),
               layer(wo, (ATTN_W, D_MODEL), (0, 0)), layer(wo, (GMLP_W, D_MODEL), (1, 0)),
               layer(vec(mix_post_norm)))
        ffn2 = (layer(vec(ffn2_pre_norm)), layer(f2g), layer(f2u), layer(f2d), layer(vec(ffn2_post_norm)))
        ple = (layer(vec(ple_pre_norm)), layer(pwg), layer(pwp), layer(vec(ple_post_norm)))
        h = _post_attention(h, attn, gut, gvt, p, i, mix, ffn2, ple)
    return h.reshape(BATCH, SEQ, D_MODEL)
```

```python
import numpy as np
import jax
import jax.numpy as jnp
from jax import lax
from jax.experimental import pallas as pl
from jax.experimental.pallas import tpu as pltpu

D_MODEL = 1024
BATCH = 16
SEQ = 2048
DEPTH = 2
HEAD_DIM = 64
N_HEADS = 8
N_GROUPS = 8
ATTN_W = N_HEADS * HEAD_DIM
GMLP_W = N_GROUPS * HEAD_DIM
BLK = 256
N_BLK = SEQ // BLK
TOP_K = 3
CHUNK = 128
D_FF = 2816
PLE_DIM = 256
RMS_EPS = 1e-6
LN_EPS = 1e-5
NEG = -1e30
TOKENS = BATCH * SEQ

PAIR_W = 2 * HEAD_DIM
TM = 512
SUB = 256
FF_CHUNK = 256
VMEM_LIMIT = 56 * 1024 * 1024

F32 = jnp.float32
BF16 = jnp.bfloat16


def _rms(x, g):
    return x * lax.rsqrt(jnp.mean(x * x, axis=-1, keepdims=True) + RMS_EPS) * g


def _gelu(x):
    return 0.5 * x * (1.0 + lax.erf(x * np.float32(np.sqrt(0.5))))


def _dot(a, b):
    return jnp.dot(a, b, preferred_element_type=F32)


def _dot_nt(a, b):
    return lax.dot_general(a, b, (((1,), (1,)), ((), ())), preferred_element_type=F32)


def _layer_operand(stacked, layer, block=None, index=None):
    block = stacked.shape[1:] if block is None else block
    index = (0,) * len(block) if index is None else index
    spec = pl.BlockSpec((None, *block), lambda *_: (layer, *index), pipeline_mode=pl.Buffered(1))
    return stacked, spec


def _params(n_axes):
    return pltpu.CompilerParams(
        dimension_semantics=("arbitrary",) * n_axes, vmem_limit_bytes=VMEM_LIMIT)


def _ffn_stage(h, rows, gpre, gpost, wg_ref, wu_ref, wd_ref, a_ref):
    hn = _rms(h, gpre).astype(BF16)
    for c in range(D_FF // FF_CHUNK):
        sl = slice(c * FF_CHUNK, (c + 1) * FF_CHUNK)
        g = _dot(hn, wg_ref[:, sl])
        u = _dot(hn, wu_ref[:, sl])
        a_ref[rows, sl] = (g * jax.nn.sigmoid(g) * u).astype(BF16)
    f = _dot(a_ref[rows, :], wd_ref[...])
    return h + 0.5 * _rms(f, gpost)


def _feature_major(wt_ref, r, hn):
    return _dot_nt(wt_ref[r * ATTN_W:(r + 1) * ATTN_W, :], hn)


def _inproj_gates(hn, wt_ref):
    gv = _gelu(_feature_major(wt_ref, 3, hn))
    return _gelu(_feature_major(wt_ref, 2, hn)), gv


def _inproj_qkv(hn, rows, wk_ref, wt_ref, k_ref, qt_ref, vt_ref):
    k_ref[rows, :] = _dot(hn, wk_ref[...]).astype(BF16)
    qt_ref[0, :, rows] = _feature_major(wt_ref, 0, hn).astype(BF16)
    vt_ref[0, :, rows] = _feature_major(wt_ref, 1, hn).astype(BF16)


def _gmlp_norm(gv, lng_ref, lnb_ref, vn_ref):
    for c in range(SUB // CHUNK):
        v = gv[:, c * CHUNK:(c + 1) * CHUNK].reshape(N_GROUPS, HEAD_DIM, CHUNK)
        mu = jnp.mean(v, axis=1, keepdims=True)
        var = jnp.mean(jnp.square(v - mu), axis=1, keepdims=True)
        vn = ((v - mu) * lax.rsqrt(var + LN_EPS)).reshape(GMLP_W, CHUNK)
        vn = (vn * lng_ref[...] + lnb_ref[...]).astype(BF16).reshape(N_GROUPS, HEAD_DIM, CHUNK)
        vn_ref[:, c * HEAD_DIM:(c + 1) * HEAD_DIM, :] = vn


def _gmlp_mix(gu, rows, w_causal, bs_ref, gn_ref, o_ref, vn_ref, gated_ref):
    n_chunks = SUB // CHUNK
    ssq = jnp.zeros((n_chunks, 1, CHUNK), F32)
    for g in range(N_GROUPS):
        mixed = _dot_nt(vn_ref[g], w_causal[g]) + bs_ref[g:g + 1, :]
        feat = slice(g * HEAD_DIM, (g + 1) * HEAD_DIM)
        u = jnp.concatenate([gu[feat, c * CHUNK:(c + 1) * CHUNK] for c in range(n_chunks)], axis=0)
        gated = (u * mixed).reshape(n_chunks, HEAD_DIM, CHUNK)
        ssq = ssq + jnp.sum(gated * gated, axis=1, keepdims=True)
        gated_ref[:, feat, :] = gated
    inv = lax.rsqrt(ssq * (1.0 / GMLP_W) + RMS_EPS)
    o_ref[rows, :] = jnp.concatenate(
        [(gated_ref[c] * inv[c] * gn_ref[...]).T for c in range(n_chunks)], axis=0).astype(BF16)


def _outproj_stage(h, rows, attn_ref, gmlp_ref, gpost, an_ref, woa_ref, wog_ref):
    attn_n = _rms(attn_ref[rows, :].astype(F32), an_ref[...]).astype(BF16)
    m = _dot(attn_n, woa_ref[...]) + _dot(gmlp_ref[rows, :], wog_ref[...])
    return h + _rms(m, gpost)


def _ple_stage(h, rows, p_ref, gpre, gpost, wg_ref, wp_ref):
    hn = _rms(h, gpre).astype(BF16)
    gate = jax.nn.sigmoid(_dot(hn, wg_ref[...]))
    e = gate * _dot(p_ref[rows, :].astype(BF16), wp_ref[...])
    return h + _rms(e, gpost)


(G_FFN1_PRE, G_FFN1_POST, G_MIX_PRE, G_MIX_POST, G_FFN2_PRE, G_FFN2_POST, G_PLE_PRE, G_PLE_POST) = range(8)
N_GAINS = 8
N_FFN_REFS = 3
N_INPROJ_REFS = 2
N_GMLP_REFS = 5
N_OUTPROJ_REFS = 3
N_PLE_REFS = 2
SUB_TILES = [slice(r * SUB, (r + 1) * SUB) for r in range(TM // SUB)]


def _split(refs, *sizes):
    groups, at = [], 0
    for n in sizes:
        groups.append(refs[at:at + n])
        at += n
    return (*groups, refs[at:])


def _gain(gains_ref, row):
    return gains_ref[row:row + 1, :]


def _pre_attention_kernel(h_ref, gains_ref, *refs):
    (ffn, (wk_ref, wt_ref), (ws_ref, lng_ref, lnb_ref, bs_ref, gn_ref),
     (h1_ref, k_ref, qt_ref, vt_ref, gmlp_ref, a_ref, vn_ref, gated_ref)) = _split(
        refs, N_FFN_REFS, N_INPROJ_REFS, N_GMLP_REFS)
    t_idx = lax.broadcasted_iota(jnp.int32, (CHUNK, CHUNK), 0)
    s_idx = lax.broadcasted_iota(jnp.int32, (CHUNK, CHUNK), 1)
    w_causal = jnp.where(s_idx <= t_idx, ws_ref[...], 0.0).astype(BF16)
    hs = [_ffn_stage(h_ref[rows, :], rows, _gain(gains_ref, G_FFN1_PRE), _gain(gains_ref, G_FFN1_POST),
                     *ffn, a_ref) for rows in SUB_TILES]
    hns, gus = [], []
    for r, (rows, h1) in enumerate(zip(SUB_TILES, hs)):
        h1_ref[rows, :] = h1
        hn = _rms(h1, _gain(gains_ref, G_MIX_PRE)).astype(BF16)
        gu, gv = _inproj_gates(hn, wt_ref)
        _gmlp_norm(gv, lng_ref, lnb_ref, vn_ref.at[r])
        hns.append(hn)
        gus.append(gu)
    for r, rows in enumerate(SUB_TILES):
        _inproj_qkv(hns[r], rows, wk_ref, wt_ref, k_ref, qt_ref, vt_ref)
        _gmlp_mix(gus[r], rows, w_causal, bs_ref, gn_ref, gmlp_ref, vn_ref.at[r], gated_ref.at[r])


def _pre_attention(h, gains, ffn, inproj, gmlp):
    tiles_per_seq = SEQ // TM
    row = pl.BlockSpec((TM, D_MODEL), lambda i: (i, 0))
    tok = pl.BlockSpec((TM, ATTN_W), lambda i: (i, 0))
    feat = pl.BlockSpec((1, ATTN_W, TM), lambda i: (i // tiles_per_seq, 0, i % tiles_per_seq))
    tok_shape = jax.ShapeDtypeStruct((TOKENS, ATTN_W), BF16)
    feat_shape = jax.ShapeDtypeStruct((BATCH, ATTN_W, SEQ), BF16)
    assert len(ffn) == N_FFN_REFS and len(inproj) == N_INPROJ_REFS and len(gmlp) == N_GMLP_REFS
    operands, specs = zip(gains, *ffn, *inproj, *gmlp)
    return pl.pallas_call(
        _pre_attention_kernel,
        grid=(TOKENS // TM,),
        in_specs=[row, *specs],
        out_specs=[row, tok, feat, feat, tok],
        out_shape=[jax.ShapeDtypeStruct((TOKENS, D_MODEL), F32), tok_shape, feat_shape, feat_shape,
                   jax.ShapeDtypeStruct((TOKENS, GMLP_W), BF16)],
        scratch_shapes=[pltpu.VMEM((TM, D_FF), BF16),
                        pltpu.VMEM((TM // SUB, N_GROUPS, (SUB // CHUNK) * HEAD_DIM, CHUNK), BF16),
                        pltpu.VMEM((TM // SUB, SUB // CHUNK, GMLP_W, CHUNK), F32)],
        compiler_params=_params(1),
        name="ffn1_inproj_gmlp",
    )(h, *operands)


def _post_attention_kernel(h_ref, attn_ref, gmlp_ref, p_ref, gains_ref, *refs):
    outproj, ffn, ple, (o_ref, a_ref) = _split(refs, N_OUTPROJ_REFS, N_FFN_REFS, N_PLE_REFS)
    hs = [_outproj_stage(h_ref[rows, :], rows, attn_ref, gmlp_ref, _gain(gains_ref, G_MIX_POST), *outproj)
          for rows in SUB_TILES]
    hs = [_ffn_stage(h, rows, _gain(gains_ref, G_FFN2_PRE), _gain(gains_ref, G_FFN2_POST), *ffn, a_ref)
          for rows, h in zip(SUB_TILES, hs)]
    for rows, h in zip(SUB_TILES, hs):
        o_ref[rows, :] = _ple_stage(h, rows, p_ref, _gain(gains_ref, G_PLE_PRE), _gain(gains_ref, G_PLE_POST),
                                    *ple)


def _post_attention(h, attn, gmlp, p, layer, gains, outproj, ffn, ple):
    row = pl.BlockSpec((TM, D_MODEL), lambda i: (i, 0))
    tok = pl.BlockSpec((TM, ATTN_W), lambda i: (i, 0))
    assert len(outproj) == N_OUTPROJ_REFS and len(ffn) == N_FFN_REFS and len(ple) == N_PLE_REFS
    operands, specs = zip(gains, *outproj, *ffn, *ple)
    return pl.pallas_call(
        _post_attention_kernel,
        grid=(TOKENS // TM,),
        in_specs=[row, tok, tok, pl.BlockSpec((None, TM, PLE_DIM), lambda i: (layer, i, 0)), *specs],
        out_specs=row,
        out_shape=jax.ShapeDtypeStruct((TOKENS, D_MODEL), F32),
        scratch_shapes=[pltpu.VMEM((TM, D_FF), BF16)],
        compiler_params=_params(1),
        name="outproj_ffn2_ple",
    )(h, attn, gmlp, p, *operands)


SPLIT = 4
POS_ROWS = 4 * SPLIT
CHOICE_ROWS = N_BLK
AUX_CHOICE = POS_ROWS
SCORE_LOOKAHEAD = 4
SUM_ROWS = 16
LOG2E = float(np.log2(np.e))


def _split_bf16(x):
    terms, rest = [], x
    for _ in range(SPLIT):
        t = rest.astype(BF16).astype(np.float64)
        terms.append(t)
        rest = rest - t
    assert np.max(np.abs(rest)) <= 1e-7 * max(1.0, np.max(np.abs(x)))
    return terms


def _attention_constants():
    start = 2.0 ** (-8.0 / N_HEADS)
    slopes = np.array([start ** (i + 1) for i in range(N_HEADS)], dtype=np.float32).astype(np.float64)
    pos = np.arange(SEQ)
    local = (pos % BLK).astype(np.float64)
    blk = pos // BLK
    kaux = np.zeros((N_HEADS, SEQ, PAIR_W), np.float64)
    qaux = np.zeros((N_HEADS, POS_ROWS, SEQ), np.float64)
    for h in range(N_HEADS):
        base = (1 - h % 2) * HEAD_DIM
        c = LOG2E * slopes[h]
        fine, coarse = _split_bf16(c * local), _split_bf16(c * BLK * blk)
        for n in range(SPLIT):
            kaux[h, :, base + n] = 1.0
            qaux[h, n] = -fine[n]
            kaux[h, :, base + SPLIT + n] = fine[n]
            qaux[h, SPLIT + n] = 1.0
            kaux[h, :, base + 2 * SPLIT + n] = 1.0
            qaux[h, 2 * SPLIT + n] = -coarse[n]
            kaux[h, :, base + 3 * SPLIT + n] = coarse[n]
            qaux[h, 3 * SPLIT + n] = 1.0
        kaux[h, pos, base + AUX_CHOICE + blk] = 1.0
    assert np.array_equal(kaux.astype(BF16).astype(np.float64), kaux)
    assert np.array_equal(qaux.astype(BF16).astype(np.float64), qaux)
    return jnp.asarray(kaux.astype(BF16)), jnp.asarray(qaux.astype(np.float32))


def _attn_kernel(kaux_ref, qaux_ref, k_ref, qt_ref, vt_ref, o_ref, kaug_ref, choice_ref, vaug_ref):
    lane = lax.broadcasted_iota(jnp.int32, (1, PAIR_W), 1)
    k_all = k_ref[...]
    q_all = qt_ref[0]
    k_mean = jnp.mean(k_all.astype(F32).reshape(N_BLK, BLK, PAIR_W), axis=1)
    blk_of_row = lax.broadcasted_iota(jnp.int32, (N_BLK, SEQ), 0)
    blk_of_query = lax.broadcasted_iota(jnp.int32, (N_BLK, SEQ), 1) // BLK
    in_past = blk_of_row < blk_of_query
    for a in range(2):
        in_head = (lane >= a * HEAD_DIM) & (lane < (a + 1) * HEAD_DIM)
        kaug_ref[a] = jnp.where(in_head, k_all, kaux_ref[a])
        km = jnp.where(in_head, k_mean, 0.0)
        km_hi = km.astype(BF16)
        km_lo = (km - km_hi.astype(F32)).astype(BF16)
        gate = _dot(km_hi, q_all) + _dot(km_lo, q_all)
        gate = jnp.where(in_past, gate, NEG)
        rank = jnp.zeros((N_BLK, SEQ), jnp.int32)
        for other in range(N_BLK):
            row = gate[other:other + 1, :]
            beats = (row > gate) | ((row == gate) & (other < blk_of_row))
            rank = rank + beats.astype(jnp.int32)
        chosen = (rank < TOP_K) & (gate > 0.5 * NEG)
        choice_ref[a] = jnp.where(in_past & jnp.logical_not(chosen), NEG, 0.0)
        vaug_ref[a, :HEAD_DIM, :] = vt_ref[0, a * HEAD_DIM:(a + 1) * HEAD_DIM, :]
        vaug_ref[a, HEAD_DIM:, :] = jnp.ones((SUM_ROWS, SEQ), BF16)

    key_idx = lax.broadcasted_iota(jnp.int32, (BLK, BLK), 0)
    query_idx = lax.broadcasted_iota(jnp.int32, (BLK, BLK), 1)
    causal = key_idx <= query_idx
    aux_fill = jnp.zeros((HEAD_DIM - POS_ROWS - CHOICE_ROWS, BLK), F32)

    def blk_slice(j):
        return slice(j * BLK, (j + 1) * BLK)

    tiles = [(i, a, j) for i in range(N_BLK) for j in [i] + list(range(i)) for a in range(2)]
    q_augs, scores, state, outs = {}, {}, {}, {}

    def issue_scores(n):
        i, a, j = tiles[n]
        if (i, a) not in q_augs:
            feat = slice(a * HEAD_DIM, (a + 1) * HEAD_DIM)
            q_head = qt_ref[0, feat, blk_slice(i)].astype(F32) * (LOG2E * HEAD_DIM ** -0.5)
            aux = jnp.concatenate(
                [qaux_ref[a, :, blk_slice(i)], choice_ref[a, :, blk_slice(i)], aux_fill], axis=0)
            q_augs[i, a] = jnp.concatenate(
                [q_head, aux] if a == 0 else [aux, q_head], axis=0).astype(BF16)
        scores[n] = _dot(kaug_ref[a, blk_slice(j), :], q_augs[i, a])

    def consume_scores(n):
        i, a, j = tiles[n]
        s = scores.pop(n)
        if j == i:
            s = jnp.where(causal, s, NEG)
            m = jnp.max(s, axis=0, keepdims=True)
            p = jnp.exp2(s - m)
            acc = _dot(vaug_ref[a, :, blk_slice(j)], p.astype(BF16))
        else:
            m_old, acc = state[i, a]
            m = jnp.maximum(m_old, jnp.max(s, axis=0, keepdims=True))
            alpha = jnp.exp2(m_old - m)
            p = jnp.exp2(s - m)
            acc = alpha * acc + _dot(vaug_ref[a, :, blk_slice(j)], p.astype(BF16))
        state[i, a] = (m, acc)
        if j == (i - 1 if i > 0 else 0):
            outs.setdefault(i, []).append(acc[:HEAD_DIM] / acc[HEAD_DIM:HEAD_DIM + 1])
            del state[i, a]
            if a == 1:
                o_ref[blk_slice(i), :] = jnp.concatenate(outs.pop(i), axis=0).T.astype(BF16)

    for n in range(min(SCORE_LOOKAHEAD, len(tiles))):
        issue_scores(n)
    for n in range(len(tiles)):
        if n + SCORE_LOOKAHEAD < len(tiles):
            issue_scores(n + SCORE_LOOKAHEAD)
        consume_scores(n)


def _attention(kaux, qaux, k, qt, vt):
    tok = pl.BlockSpec((SEQ, PAIR_W), lambda b, p: (b, p))
    feat = pl.BlockSpec((1, PAIR_W, SEQ), lambda b, p: (b, p, 0))
    return pl.pallas_call(
        _attn_kernel,
        grid=(BATCH, ATTN_W // PAIR_W),
        in_specs=[pl.BlockSpec((2, SEQ, PAIR_W), lambda b, p: (p, 0, 0)),
                  pl.BlockSpec((2, POS_ROWS, SEQ), lambda b, p: (p, 0, 0)),
                  tok, feat, feat],
        out_specs=tok,
        out_shape=jax.ShapeDtypeStruct((TOKENS, ATTN_W), BF16),
        scratch_shapes=[pltpu.VMEM((2, SEQ, PAIR_W), BF16),
                        pltpu.VMEM((2, N_BLK, SEQ), F32),
                        pltpu.VMEM((2, HEAD_DIM + SUM_ROWS, SEQ), BF16)],
        compiler_params=_params(2),
        name="moba_attention",
    )(kaux, qaux, k, qt, vt)


CAST_ROWS = 256


def _cast_kernel(*refs):
    n = len(refs) // 2
    for src, dst in zip(refs[:n], refs[n:]):
        dst[...] = src[...].astype(BF16)


def _cast_bf16(*stacked):
    depth, rows, cols = stacked[0].shape
    assert all(w.shape == stacked[0].shape for w in stacked) and rows % CAST_ROWS == 0
    spec = pl.BlockSpec((None, CAST_ROWS, cols), lambda d, r: (d, r, 0))
    return pl.pallas_call(
        _cast_kernel,
        grid=(depth, rows // CAST_ROWS),
        in_specs=[spec] * len(stacked),
        out_specs=[spec] * len(stacked),
        out_shape=[jax.ShapeDtypeStruct(w.shape, BF16) for w in stacked],
        compiler_params=_params(2),
        name="cast_bf16",
    )(*stacked)


def _transpose_cast_kernel(w_ref, o_ref):
    o_ref[...] = w_ref[...].T.astype(BF16)


def _inproj_weights_transposed(w_in):
    return pl.pallas_call(
        _transpose_cast_kernel,
        grid=(DEPTH, 4),
        in_specs=[pl.BlockSpec((None, D_MODEL, ATTN_W), lambda d, r: (d, 0, r + jnp.minimum(r, 1)))],
        out_specs=pl.BlockSpec((None, ATTN_W, D_MODEL), lambda d, r: (d, r, 0)),
        out_shape=jax.ShapeDtypeStruct((DEPTH, 4 * ATTN_W, D_MODEL), BF16),
        compiler_params=_params(2),
        name="inproj_weights_t",
    )(w_in)


def kernel(x, p, ffn1_pre_norm, ffn1_w_gate, ffn1_w_up, ffn1_w_down, ffn1_post_norm, mix_pre_norm, w_in, gmlp_ln_g, gmlp_ln_b, gmlp_w_s, gmlp_b_s, attn_out_norm, gmlp_out_norm, w_out, mix_post_norm, ffn2_pre_norm, ffn2_w_gate, ffn2_w_up, ffn2_w_down, ffn2_post_norm, ple_pre_norm, ple_w_gate, ple_w_proj, ple_post_norm):
    assert x.shape == (BATCH, SEQ, D_MODEL) and p.shape == (DEPTH, BATCH, SEQ, PLE_DIM)
    assert ATTN_W == GMLP_W and w_in.shape == (DEPTH, D_MODEL, 5 * ATTN_W)
    kaux, qaux = _attention_constants()
    h = x.reshape(TOKENS, D_MODEL)
    p = p.reshape(DEPTH, TOKENS, PLE_DIM)

    def col(v):
        return jnp.broadcast_to(v[:, :, None], (*v.shape, CHUNK))

    gains = jnp.stack([ffn1_pre_norm, ffn1_post_norm, mix_pre_norm, mix_post_norm,
                       ffn2_pre_norm, ffn2_post_norm, ple_pre_norm, ple_post_norm], axis=1)
    assert gains.shape == (DEPTH, N_GAINS, D_MODEL)
    f1g, f1u, f2g, f2u = _cast_bf16(ffn1_w_gate, ffn1_w_up, ffn2_w_gate, ffn2_w_up)
    f1d, f2d = _cast_bf16(ffn1_w_down, ffn2_w_down)
    wo, pwg = _cast_bf16(w_out, ple_w_gate)
    w_t = _inproj_weights_transposed(w_in)
    w_k = w_in[:, :, ATTN_W:2 * ATTN_W].astype(BF16)
    pwp = ple_w_proj.astype(BF16)
    ln_g, ln_b, gn = col(gmlp_ln_g), col(gmlp_ln_b), col(gmlp_out_norm)
    attn_gain = attn_out_norm.reshape(DEPTH, 1, ATTN_W)

    for i in range(DEPTH):
        def layer(w, block=None, index=None):
            return _layer_operand(w, i, block, index)

        gmlp = (layer(gmlp_w_s), layer(ln_g), layer(ln_b), layer(gmlp_b_s), layer(gn))
        h, k, qt, vt, gmlp_n = _pre_attention(
            h, layer(gains), (layer(f1g), layer(f1u), layer(f1d)), (layer(w_k), layer(w_t)), gmlp)
        attn = _attention(kaux, qaux, k, qt, vt)
        outproj = (layer(attn_gain), layer(wo, (ATTN_W, D_MODEL), (0, 0)), layer(wo, (GMLP_W, D_MODEL), (1, 0)))
        h = _post_attention(h, attn, gmlp_n, p, i, layer(gains), outproj,
                            (layer(f2g), layer(f2u), layer(f2d)), (layer(pwg), layer(pwp)))
    return h.reshape(BATCH, SEQ, D_MODEL)
```

```python
import numpy as np
import jax
import jax.numpy as jnp
from jax import lax
from jax.experimental import pallas as pl
from jax.experimental.pallas import tpu as pltpu

D_MODEL = 1024
BATCH = 16
SEQ = 2048
DEPTH = 2
HEAD_DIM = 64
N_HEADS = 8
N_GROUPS = 8
ATTN_W = N_HEADS * HEAD_DIM
GMLP_W = N_GROUPS * HEAD_DIM
BLK = 256
N_BLK = SEQ // BLK
TOP_K = 3
CHUNK = 128
D_FF = 2816
PLE_DIM = 256
RMS_EPS = 1e-6
LN_EPS = 1e-5
NEG = -1e30
TOKENS = BATCH * SEQ

PAIR_W = 2 * HEAD_DIM
TM = 1024
SUB = 256
FF_CHUNK = 256
VMEM_LIMIT = 56 * 1024 * 1024

F32 = jnp.float32
BF16 = jnp.bfloat16


def _rms(x, g):
    return x * lax.rsqrt(jnp.mean(x * x, axis=-1, keepdims=True) + RMS_EPS) * g


def _gelu(x):
    return 0.5 * x * (1.0 + lax.erf(x * np.float32(np.sqrt(0.5))))


def _dot(a, b):
    return jnp.dot(a, b, preferred_element_type=F32)


def _dot_nt(a, b):
    return lax.dot_general(a, b, (((1,), (1,)), ((), ())), preferred_element_type=F32)


def _layer_operand(stacked, layer, block=None, index=None):
    block = stacked.shape[1:] if block is None else block
    index = (0,) * len(block) if index is None else index
    spec = pl.BlockSpec((None, *block), lambda *_: (layer, *index), pipeline_mode=pl.Buffered(1))
    return stacked, spec


def _params(n_axes):
    return pltpu.CompilerParams(
        dimension_semantics=("arbitrary",) * n_axes, vmem_limit_bytes=VMEM_LIMIT)


def _ffn_stage(h, gpre, gpost, wg_ref, wu_ref, wd_ref, a_ref):
    hn = _rms(h, gpre).astype(BF16)
    for c in range(D_FF // FF_CHUNK):
        sl = slice(c * FF_CHUNK, (c + 1) * FF_CHUNK)
        g = _dot(hn, wg_ref[:, sl])
        u = _dot(hn, wu_ref[:, sl])
        a_ref[:, sl] = (g * jax.nn.sigmoid(g) * u).astype(BF16)
    f = _dot(a_ref[...], wd_ref[...])
    return h + 0.5 * _rms(f, gpost)


def _feature_major(wt_ref, r, hn):
    return _dot_nt(wt_ref[r * ATTN_W:(r + 1) * ATTN_W, :], hn)


def _inproj_gates(hn, wt_ref):
    gv = _gelu(_feature_major(wt_ref, 3, hn))
    return _gelu(_feature_major(wt_ref, 2, hn)), gv


def _inproj_qkv(hn, rows, wk_ref, wt_ref, k_ref, qt_ref, vt_ref):
    k_ref[rows, :] = _dot(hn, wk_ref[...]).astype(BF16)
    qt_ref[0, :, rows] = _feature_major(wt_ref, 0, hn).astype(BF16)
    vt_ref[0, :, rows] = _feature_major(wt_ref, 1, hn).astype(BF16)


def _gmlp_norm(gv, lng_ref, lnb_ref, vn_ref):
    for c in range(SUB // CHUNK):
        v = gv[:, c * CHUNK:(c + 1) * CHUNK].reshape(N_GROUPS, HEAD_DIM, CHUNK)
        mu = jnp.mean(v, axis=1, keepdims=True)
        var = jnp.mean(jnp.square(v - mu), axis=1, keepdims=True)
        vn = ((v - mu) * lax.rsqrt(var + LN_EPS)).reshape(GMLP_W, CHUNK)
        vn = (vn * lng_ref[...] + lnb_ref[...]).astype(BF16).reshape(N_GROUPS, HEAD_DIM, CHUNK)
        vn_ref[:, c * HEAD_DIM:(c + 1) * HEAD_DIM, :] = vn


def _gmlp_mix(gu, rows, w_causal, bs_ref, gn_ref, o_ref, vn_ref, gated_ref):
    n_chunks = SUB // CHUNK
    ssq = jnp.zeros((n_chunks, 1, CHUNK), F32)
    for g in range(N_GROUPS):
        mixed = _dot_nt(vn_ref[g], w_causal[g]) + bs_ref[g:g + 1, :]
        feat = slice(g * HEAD_DIM, (g + 1) * HEAD_DIM)
        u = jnp.concatenate([gu[feat, c * CHUNK:(c + 1) * CHUNK] for c in range(n_chunks)], axis=0)
        gated = (u * mixed).reshape(n_chunks, HEAD_DIM, CHUNK)
        ssq = ssq + jnp.sum(gated * gated, axis=1, keepdims=True)
        gated_ref[:, feat, :] = gated
    inv = lax.rsqrt(ssq * (1.0 / GMLP_W) + RMS_EPS)
    o_ref[rows, :] = jnp.concatenate(
        [(gated_ref[c] * inv[c] * gn_ref[...]).T for c in range(n_chunks)], axis=0).astype(BF16)


def _outproj_stage(h, rows, attn_ref, gmlp_ref, gpost, an_ref, woa_ref, wog_ref):
    attn_n = _rms(attn_ref[rows, :].astype(F32), an_ref[...]).astype(BF16)
    m = _dot(attn_n, woa_ref[...]) + _dot(gmlp_ref[rows, :], wog_ref[...])
    return h + _rms(m, gpost)


def _ple_stage(h, rows, p_ref, gpre, gpost, wg_ref, wp_ref):
    hn = _rms(h, gpre).astype(BF16)
    gate = jax.nn.sigmoid(_dot(hn, wg_ref[...]))
    e = gate * _dot(p_ref[rows, :].astype(BF16), wp_ref[...])
    return h + _rms(e, gpost)


(G_FFN1_PRE, G_FFN1_POST, G_MIX_PRE, G_MIX_POST, G_FFN2_PRE, G_FFN2_POST, G_PLE_PRE, G_PLE_POST) = range(8)
N_GAINS = 8
N_FFN_REFS = 3
N_INPROJ_REFS = 2
N_GMLP_REFS = 5
N_OUTPROJ_REFS = 3
N_PLE_REFS = 2
SUB_TILES = [slice(r * SUB, (r + 1) * SUB) for r in range(TM // SUB)]
A_SLOTS = 2


def _split(refs, *sizes):
    groups, at = [], 0
    for n in sizes:
        groups.append(refs[at:at + n])
        at += n
    return (*groups, refs[at:])


def _gain(gains_ref, row):
    return gains_ref[row:row + 1, :]


def _pre_attention_kernel(h_ref, gains_ref, *refs):
    (ffn, (wk_ref, wt_ref), (ws_ref, lng_ref, lnb_ref, bs_ref, gn_ref),
     (h1_ref, k_ref, qt_ref, vt_ref, gmlp_ref, a_ref, vn_ref, gated_ref)) = _split(
        refs, N_FFN_REFS, N_INPROJ_REFS, N_GMLP_REFS)
    t_idx = lax.broadcasted_iota(jnp.int32, (CHUNK, CHUNK), 0)
    s_idx = lax.broadcasted_iota(jnp.int32, (CHUNK, CHUNK), 1)
    w_causal = jnp.where(s_idx <= t_idx, ws_ref[...], 0.0).astype(BF16)
    hs = [_ffn_stage(h_ref[rows, :], _gain(gains_ref, G_FFN1_PRE), _gain(gains_ref, G_FFN1_POST),
                     *ffn, a_ref.at[r % A_SLOTS]) for r, rows in enumerate(SUB_TILES)]
    hns, gus = [], []
    for r, (rows, h1) in enumerate(zip(SUB_TILES, hs)):
        h1_ref[rows, :] = h1
        hn = _rms(h1, _gain(gains_ref, G_MIX_PRE)).astype(BF16)
        gu, gv = _inproj_gates(hn, wt_ref)
        _gmlp_norm(gv, lng_ref, lnb_ref, vn_ref.at[r])
        hns.append(hn)
        gus.append(gu)
    for r, rows in enumerate(SUB_TILES):
        _inproj_qkv(hns[r], rows, wk_ref, wt_ref, k_ref, qt_ref, vt_ref)
        _gmlp_mix(gus[r], rows, w_causal, bs_ref, gn_ref, gmlp_ref, vn_ref.at[r], gated_ref.at[r])


def _pre_attention(h, gains, ffn, inproj, gmlp):
    tiles_per_seq = SEQ // TM
    row = pl.BlockSpec((TM, D_MODEL), lambda i: (i, 0))
    tok = pl.BlockSpec((TM, ATTN_W), lambda i: (i, 0))
    feat = pl.BlockSpec((1, ATTN_W, TM), lambda i: (i // tiles_per_seq, 0, i % tiles_per_seq))
    tok_shape = jax.ShapeDtypeStruct((TOKENS, ATTN_W), BF16)
    feat_shape = jax.ShapeDtypeStruct((BATCH, ATTN_W, SEQ), BF16)
    assert len(ffn) == N_FFN_REFS and len(inproj) == N_INPROJ_REFS and len(gmlp) == N_GMLP_REFS
    operands, specs = zip(gains, *ffn, *inproj, *gmlp)
    return pl.pallas_call(
        _pre_attention_kernel,
        grid=(TOKENS // TM,),
        in_specs=[row, *specs],
        out_specs=[row, tok, feat, feat, tok],
        out_shape=[jax.ShapeDtypeStruct((TOKENS, D_MODEL), F32), tok_shape, feat_shape, feat_shape,
                   jax.ShapeDtypeStruct((TOKENS, GMLP_W), BF16)],
        scratch_shapes=[pltpu.VMEM((A_SLOTS, SUB, D_FF), BF16),
                        pltpu.VMEM((TM // SUB, N_GROUPS, (SUB // CHUNK) * HEAD_DIM, CHUNK), BF16),
                        pltpu.VMEM((TM // SUB, SUB // CHUNK, GMLP_W, CHUNK), F32)],
        compiler_params=_params(1),
        name="ffn1_inproj_gmlp",
    )(h, *operands)


def _post_attention_kernel(h_ref, attn_ref, gmlp_ref, p_ref, gains_ref, *refs):
    outproj, ffn, ple, (o_ref, a_ref) = _split(refs, N_OUTPROJ_REFS, N_FFN_REFS, N_PLE_REFS)
    hs = [_outproj_stage(h_ref[rows, :], rows, attn_ref, gmlp_ref, _gain(gains_ref, G_MIX_POST), *outproj)
          for rows in SUB_TILES]
    hs = [_ffn_stage(h, _gain(gains_ref, G_FFN2_PRE), _gain(gains_ref, G_FFN2_POST), *ffn, a_ref.at[r % A_SLOTS])
          for r, h in enumerate(hs)]
    for rows, h in zip(SUB_TILES, hs):
        o_ref[rows, :] = _ple_stage(h, rows, p_ref, _gain(gains_ref, G_PLE_PRE), _gain(gains_ref, G_PLE_POST),
                                    *ple)


def _post_attention(h, attn, gmlp, p, layer, gains, outproj, ffn, ple):
    row = pl.BlockSpec((TM, D_MODEL), lambda i: (i, 0))
    tok = pl.BlockSpec((TM, ATTN_W), lambda i: (i, 0))
    assert len(outproj) == N_OUTPROJ_REFS and len(ffn) == N_FFN_REFS and len(ple) == N_PLE_REFS
    operands, specs = zip(gains, *outproj, *ffn, *ple)
    return pl.pallas_call(
        _post_attention_kernel,
        grid=(TOKENS // TM,),
        in_specs=[row, tok, tok, pl.BlockSpec((None, TM, PLE_DIM), lambda i: (layer, i, 0)), *specs],
        out_specs=row,
        out_shape=jax.ShapeDtypeStruct((TOKENS, D_MODEL), F32),
        scratch_shapes=[pltpu.VMEM((A_SLOTS, SUB, D_FF), BF16)],
        compiler_params=_params(1),
        name="outproj_ffn2_ple",
    )(h, attn, gmlp, p, *operands)


SPLIT = 4
POS_ROWS = 4 * SPLIT
CHOICE_ROWS = N_BLK
AUX_CHOICE = POS_ROWS
SCORE_LOOKAHEAD = 4
SUM_ROWS = 16
LOG2E = float(np.log2(np.e))


def _split_bf16(x):
    terms, rest = [], x
    for _ in range(SPLIT):
        t = rest.astype(BF16).astype(np.float64)
        terms.append(t)
        rest = rest - t
    assert np.max(np.abs(rest)) <= 1e-7 * max(1.0, np.max(np.abs(x)))
    return terms


def _attention_constants():
    start = 2.0 ** (-8.0 / N_HEADS)
    slopes = np.array([start ** (i + 1) for i in range(N_HEADS)], dtype=np.float32).astype(np.float64)
    pos = np.arange(SEQ)
    local = (pos % BLK).astype(np.float64)
    blk = pos // BLK
    kaux = np.zeros((N_HEADS, SEQ, PAIR_W), np.float64)
    qaux = np.zeros((N_HEADS, POS_ROWS, SEQ), np.float64)
    for h in range(N_HEADS):
        base = (1 - h % 2) * HEAD_DIM
        c = LOG2E * slopes[h]
        fine, coarse = _split_bf16(c * local), _split_bf16(c * BLK * blk)
        for n in range(SPLIT):
            kaux[h, :, base + n] = 1.0
            qaux[h, n] = -fine[n]
            kaux[h, :, base + SPLIT + n] = fine[n]
            qaux[h, SPLIT + n] = 1.0
            kaux[h, :, base + 2 * SPLIT + n] = 1.0
            qaux[h, 2 * SPLIT + n] = -coarse[n]
            kaux[h, :, base + 3 * SPLIT + n] = coarse[n]
            qaux[h, 3 * SPLIT + n] = 1.0
        kaux[h, pos, base + AUX_CHOICE + blk] = 1.0
    assert np.array_equal(kaux.astype(BF16).astype(np.float64), kaux)
    assert np.array_equal(qaux.astype(BF16).astype(np.float64), qaux)
    return jnp.asarray(kaux.astype(BF16)), jnp.asarray(qaux.astype(np.float32))


def _attn_kernel(kaux_ref, qaux_ref, k_ref, qt_ref, vt_ref, o_ref, kaug_ref, choice_ref, vaug_ref):
    lane = lax.broadcasted_iota(jnp.int32, (1, PAIR_W), 1)
    k_all = k_ref[...]
    q_all = qt_ref[0]
    k_mean = jnp.mean(k_all.astype(F32).reshape(N_BLK, BLK, PAIR_W), axis=1)
    blk_of_row = lax.broadcasted_iota(jnp.int32, (N_BLK, SEQ), 0)
    blk_of_query = lax.broadcasted_iota(jnp.int32, (N_BLK, SEQ), 1) // BLK
    in_past = blk_of_row < blk_of_query
    for a in range(2):
        in_head = (lane >= a * HEAD_DIM) & (lane < (a + 1) * HEAD_DIM)
        kaug_ref[a] = jnp.where(in_head, k_all, kaux_ref[a])
        km = jnp.where(in_head, k_mean, 0.0)
        km_hi = km.astype(BF16)
        km_lo = (km - km_hi.astype(F32)).astype(BF16)
        gate = _dot(km_hi, q_all) + _dot(km_lo, q_all)
        gate = jnp.where(in_past, gate, NEG)
        rank = jnp.zeros((N_BLK, SEQ), jnp.int32)
        for other in range(N_BLK):
            row = gate[other:other + 1, :]
            beats = (row > gate) | ((row == gate) & (other < blk_of_row))
            rank = rank + beats.astype(jnp.int32)
        chosen = (rank < TOP_K) & (gate > 0.5 * NEG)
        choice_ref[a] = jnp.where(in_past & jnp.logical_not(chosen), NEG, 0.0)
        vaug_ref[a, :HEAD_DIM, :] = vt_ref[0, a * HEAD_DIM:(a + 1) * HEAD_DIM, :]
        vaug_ref[a, HEAD_DIM:, :] = jnp.ones((SUM_ROWS, SEQ), BF16)

    key_idx = lax.broadcasted_iota(jnp.int32, (BLK, BLK), 0)
    query_idx = lax.broadcasted_iota(jnp.int32, (BLK, BLK), 1)
    causal = key_idx <= query_idx
    aux_fill = jnp.zeros((HEAD_DIM - POS_ROWS - CHOICE_ROWS, BLK), F32)

    def blk_slice(j):
        return slice(j * BLK, (j + 1) * BLK)

    tiles = [(i, a, j) for i in range(N_BLK) for j in [i] + list(range(i)) for a in range(2)]
    q_augs, scores, state, outs = {}, {}, {}, {}

    def issue_scores(n):
        i, a, j = tiles[n]
        if (i, a) not in q_augs:
            feat = slice(a * HEAD_DIM, (a + 1) * HEAD_DIM)
            q_head = qt_ref[0, feat, blk_slice(i)].astype(F32) * (LOG2E * HEAD_DIM ** -0.5)
            aux = jnp.concatenate(
                [qaux_ref[a, :, blk_slice(i)], choice_ref[a, :, blk_slice(i)], aux_fill], axis=0)
            q_augs[i, a] = jnp.concatenate(
                [q_head, aux] if a == 0 else [aux, q_head], axis=0).astype(BF16)
        scores[n] = _dot(kaug_ref[a, blk_slice(j), :], q_augs[i, a])

    def consume_scores(n):
        i, a, j = tiles[n]
        s = scores.pop(n)
        if j == i:
            s = jnp.where(causal, s, NEG)
            m = jnp.max(s, axis=0, keepdims=True)
            p = jnp.exp2(s - m)
            acc = _dot(vaug_ref[a, :, blk_slice(j)], p.astype(BF16))
        else:
            m_old, acc = state[i, a]
            m = jnp.maximum(m_old, jnp.max(s, axis=0, keepdims=True))
            alpha = jnp.exp2(m_old - m)
            p = jnp.exp2(s - m)
            acc = alpha * acc + _dot(vaug_ref[a, :, blk_slice(j)], p.astype(BF16))
        state[i, a] = (m, acc)
        if j == (i - 1 if i > 0 else 0):
            outs.setdefault(i, []).append(acc[:HEAD_DIM] / acc[HEAD_DIM:HEAD_DIM + 1])
            del state[i, a]
            if a == 1:
                o_ref[blk_slice(i), :] = jnp.concatenate(outs.pop(i), axis=0).T.astype(BF16)

    for n in range(min(SCORE_LOOKAHEAD, len(tiles))):
        issue_scores(n)
    for n in range(len(tiles)):
        if n + SCORE_LOOKAHEAD < len(tiles):
            issue_scores(n + SCORE_LOOKAHEAD)
        consume_scores(n)


def _attention(kaux, qaux, k, qt, vt):
    tok = pl.BlockSpec((SEQ, PAIR_W), lambda b, p: (b, p))
    feat = pl.BlockSpec((1, PAIR_W, SEQ), lambda b, p: (b, p, 0))
    return pl.pallas_call(
        _attn_kernel,
        grid=(BATCH, ATTN_W // PAIR_W),
        in_specs=[pl.BlockSpec((2, SEQ, PAIR_W), lambda b, p: (p, 0, 0)),
                  pl.BlockSpec((2, POS_ROWS, SEQ), lambda b, p: (p, 0, 0)),
                  tok, feat, feat],
        out_specs=tok,
        out_shape=jax.ShapeDtypeStruct((TOKENS, ATTN_W), BF16),
        scratch_shapes=[pltpu.VMEM((2, SEQ, PAIR_W), BF16),
                        pltpu.VMEM((2, N_BLK, SEQ), F32),
                        pltpu.VMEM((2, HEAD_DIM + SUM_ROWS, SEQ), BF16)],
        compiler_params=_params(2),
        name="moba_attention",
    )(kaux, qaux, k, qt, vt)


CAST_ROWS = 256


def _cast_kernel(*refs):
    n = len(refs) // 2
    for src, dst in zip(refs[:n], refs[n:]):
        dst[...] = src[...].astype(BF16)


def _cast_bf16(*stacked):
    depth, rows, cols = stacked[0].shape
    assert all(w.shape == stacked[0].shape for w in stacked) and rows % CAST_ROWS == 0
    spec = pl.BlockSpec((None, CAST_ROWS, cols), lambda d, r: (d, r, 0))
    return pl.pallas_call(
        _cast_kernel,
        grid=(depth, rows // CAST_ROWS),
        in_specs=[spec] * len(stacked),
        out_specs=[spec] * len(stacked),
        out_shape=[jax.ShapeDtypeStruct(w.shape, BF16) for w in stacked],
        compiler_params=_params(2),
        name="cast_bf16",
    )(*stacked)


def _transpose_cast_kernel(w_ref, o_ref):
    o_ref[...] = w_ref[...].T.astype(BF16)


def _inproj_weights_transposed(w_in):
    return pl.pallas_call(
        _transpose_cast_kernel,
        grid=(DEPTH, 4),
        in_specs=[pl.BlockSpec((None, D_MODEL, ATTN_W), lambda d, r: (d, 0, r + jnp.minimum(r, 1)))],
        out_specs=pl.BlockSpec((None, ATTN_W, D_MODEL), lambda d, r: (d, r, 0)),
        out_shape=jax.ShapeDtypeStruct((DEPTH, 4 * ATTN_W, D_MODEL), BF16),
        compiler_params=_params(2),
        name="inproj_weights_t",
    )(w_in)


def kernel(x, p, ffn1_pre_norm, ffn1_w_gate, ffn1_w_up, ffn1_w_down, ffn1_post_norm, mix_pre_norm, w_in, gmlp_ln_g, gmlp_ln_b, gmlp_w_s, gmlp_b_s, attn_out_norm, gmlp_out_norm, w_out, mix_post_norm, ffn2_pre_norm, ffn2_w_gate, ffn2_w_up, ffn2_w_down, ffn2_post_norm, ple_pre_norm, ple_w_gate, ple_w_proj, ple_post_norm):
    assert x.shape == (BATCH, SEQ, D_MODEL) and p.shape == (DEPTH, BATCH, SEQ, PLE_DIM)
    assert ATTN_W == GMLP_W and w_in.shape == (DEPTH, D_MODEL, 5 * ATTN_W)
    kaux, qaux = _attention_constants()
    h = x.reshape(TOKENS, D_MODEL)
    p = p.reshape(DEPTH, TOKENS, PLE_DIM)

    def col(v):
        return jnp.broadcast_to(v[:, :, None], (*v.shape, CHUNK))

    gains = jnp.stack([ffn1_pre_norm, ffn1_post_norm, mix_pre_norm, mix_post_norm,
                       ffn2_pre_norm, ffn2_post_norm, ple_pre_norm, ple_post_norm], axis=1)
    assert gains.shape == (DEPTH, N_GAINS, D_MODEL)
    f1g, f1u, f2g, f2u = _cast_bf16(ffn1_w_gate, ffn1_w_up, ffn2_w_gate, ffn2_w_up)
    f1d, f2d = _cast_bf16(ffn1_w_down, ffn2_w_down)
    wo, pwg = _cast_bf16(w_out, ple_w_gate)
    w_t = _inproj_weights_transposed(w_in)
    w_k = w_in[:, :, ATTN_W:2 * ATTN_W].astype(BF16)
    pwp = ple_w_proj.astype(BF16)
    ln_g, ln_b, gn = col(gmlp_ln_g), col(gmlp_ln_b), col(gmlp_out_norm)
    attn_gain = attn_out_norm.reshape(DEPTH, 1, ATTN_W)

    for i in range(DEPTH):
        def layer(w, block=None, index=None):
            return _layer_operand(w, i, block, index)

        gmlp = (layer(gmlp_w_s), layer(ln_g), layer(ln_b), layer(gmlp_b_s), layer(gn))
        h, k, qt, vt, gmlp_n = _pre_attention(
            h, layer(gains), (layer(f1g), layer(f1u), layer(f1d)), (layer(w_k), layer(w_t)), gmlp)
        attn = _attention(kaux, qaux, k, qt, vt)
        outproj = (layer(attn_gain), layer(wo, (ATTN_W, D_MODEL), (0, 0)), layer(wo, (GMLP_W, D_MODEL), (1, 0)))
        h = _post_attention(h, attn, gmlp_n, p, i, layer(gains), outproj,
                            (layer(f2g), layer(f2u), layer(f2d)), (layer(pwg), layer(pwp)))
    return h.reshape(BATCH, SEQ, D_MODEL)
```

```python
import functools

import numpy as np
import jax
import jax.numpy as jnp
from jax import lax
from jax.experimental import pallas as pl
from jax.experimental.pallas import tpu as pltpu

D_MODEL = 1024
BATCH = 16
SEQ = 2048
DEPTH = 2
HEAD_DIM = 64
N_HEADS = 8
N_GROUPS = 8
ATTN_W = N_HEADS * HEAD_DIM
GMLP_W = N_GROUPS * HEAD_DIM
BLK = 256
N_BLK = SEQ // BLK
TOP_K = 3
CHUNK = 128
D_FF = 2816
PLE_DIM = 256
RMS_EPS = 1e-6
LN_EPS = 1e-5
NEG = -1e30
TOKENS = BATCH * SEQ

PAIR_W = 2 * HEAD_DIM
TM = 512
SUB = 256
FF_CHUNK = 256
VMEM_LIMIT = 56 * 1024 * 1024

F32 = jnp.float32
BF16 = jnp.bfloat16


def _rms(x, g):
    return x * lax.rsqrt(jnp.mean(x * x, axis=-1, keepdims=True) + RMS_EPS) * g


def _gelu(x):
    return 0.5 * x * (1.0 + lax.erf(x * np.float32(np.sqrt(0.5))))


def _dot(a, b):
    return jnp.dot(a, b, preferred_element_type=F32)


def _dot_nt(a, b):
    return lax.dot_general(a, b, (((1,), (1,)), ((), ())), preferred_element_type=F32)


def _layer_operand(stacked, layer, block=None, index=None):
    block = stacked.shape[1:] if block is None else block
    index = (0,) * len(block) if index is None else index
    spec = pl.BlockSpec((None, *block), lambda *_: (layer, *index), pipeline_mode=pl.Buffered(1))
    return stacked, spec


def _params(n_axes):
    return pltpu.CompilerParams(
        dimension_semantics=("arbitrary",) * n_axes, vmem_limit_bytes=VMEM_LIMIT)


def _ffn_up(hn, chunks, wg_ref, wu_ref, a_ref):
    for c in chunks:
        sl = slice(c * FF_CHUNK, (c + 1) * FF_CHUNK)
        g = _dot(hn, wg_ref[:, sl])
        u = _dot(hn, wu_ref[:, sl])
        a_ref[:, sl] = (g * jax.nn.sigmoid(g) * u).astype(BF16)


def _ffn_pair(hs, gpre, gpost, wg_ref, wu_ref, wd_ref, a_ref, before_second_down):
    n_chunks = D_FF // FF_CHUNK
    hn0, hn1 = [_rms(h, gpre).astype(BF16) for h in hs]
    _ffn_up(hn0, range(n_chunks), wg_ref, wu_ref, a_ref.at[0])
    _ffn_up(hn1, range(FFN_AHEAD), wg_ref, wu_ref, a_ref.at[1])
    out0 = hs[0] + 0.5 * _rms(_dot(a_ref[0], wd_ref[...]), gpost)
    _ffn_up(hn1, range(FFN_AHEAD, n_chunks), wg_ref, wu_ref, a_ref.at[1])
    filler = before_second_down(out0)
    out1 = hs[1] + 0.5 * _rms(_dot(a_ref[1], wd_ref[...]), gpost)
    return out0, out1, filler


def _feature_major(wt_ref, r, hn):
    return _dot_nt(wt_ref[r * ATTN_W:(r + 1) * ATTN_W, :], hn)


def _inproj_gates(hn, wt_ref):
    gv = _gelu(_feature_major(wt_ref, 3, hn))
    return _gelu(_feature_major(wt_ref, 2, hn)), gv


def _inproj_qkv(hn, rows, wk_ref, wt_ref, k_ref, qt_ref, vt_ref):
    k_ref[rows, :] = _dot(hn, wk_ref[...]).astype(BF16)
    qt_ref[0, :, rows] = _feature_major(wt_ref, 0, hn).astype(BF16)
    vt_ref[0, :, rows] = _feature_major(wt_ref, 1, hn).astype(BF16)


def _gmlp_norm(gv, lng_ref, lnb_ref, vn_ref):
    for c in range(SUB // CHUNK):
        v = gv[:, c * CHUNK:(c + 1) * CHUNK].reshape(N_GROUPS, HEAD_DIM, CHUNK)
        mu = jnp.mean(v, axis=1, keepdims=True)
        var = jnp.mean(jnp.square(v - mu), axis=1, keepdims=True)
        vn = ((v - mu) * lax.rsqrt(var + LN_EPS)).reshape(GMLP_W, CHUNK)
        vn = (vn * lng_ref[...] + lnb_ref[...]).astype(BF16).reshape(N_GROUPS, HEAD_DIM, CHUNK)
        vn_ref[:, c * HEAD_DIM:(c + 1) * HEAD_DIM, :] = vn


def _gmlp_mix(gu, rows, w_causal, bs_ref, gn_ref, o_ref, vn_ref, gated_ref):
    n_chunks = SUB // CHUNK
    ssq = jnp.zeros((n_chunks, 1, CHUNK), F32)
    for g in range(N_GROUPS):
        mixed = _dot_nt(vn_ref[g], w_causal[g]) + bs_ref[g:g + 1, :]
        feat = slice(g * HEAD_DIM, (g + 1) * HEAD_DIM)
        u = jnp.concatenate([gu[feat, c * CHUNK:(c + 1) * CHUNK] for c in range(n_chunks)], axis=0)
        gated = (u * mixed).reshape(n_chunks, HEAD_DIM, CHUNK)
        ssq = ssq + jnp.sum(gated * gated, axis=1, keepdims=True)
        gated_ref[:, feat, :] = gated
    inv = lax.rsqrt(ssq * (1.0 / GMLP_W) + RMS_EPS)
    o_ref[rows, :] = jnp.concatenate(
        [(gated_ref[c] * inv[c] * gn_ref[...]).T for c in range(n_chunks)], axis=0).astype(BF16)


def _outproj_stage(h, rows, attn_ref, gmlp_ref, gpost, an_ref, woa_ref, wog_ref):
    attn_n = _rms(attn_ref[rows, :].astype(F32), an_ref[...]).astype(BF16)
    m = _dot(attn_n, woa_ref[...]) + _dot(gmlp_ref[rows, :], wog_ref[...])
    return h + _rms(m, gpost)


def _ple_stage(h, p_proj, gpre, gpost, wg_ref):
    hn = _rms(h, gpre).astype(BF16)
    gate = jax.nn.sigmoid(_dot(hn, wg_ref[...]))
    return h + _rms(gate * p_proj, gpost)


(G_FFN1_PRE, G_FFN1_POST, G_MIX_PRE, G_MIX_POST, G_FFN2_PRE, G_FFN2_POST, G_PLE_PRE, G_PLE_POST) = range(8)
N_GAINS = 8
N_FFN_REFS = 3
N_INPROJ_REFS = 2
N_GMLP_REFS = 5
N_OUTPROJ_REFS = 3
N_PLE_REFS = 2
SUB_TILES = [slice(r * SUB, (r + 1) * SUB) for r in range(TM // SUB)]
assert len(SUB_TILES) == 2
FFN_AHEAD = 2


def _split(refs, *sizes):
    groups, at = [], 0
    for n in sizes:
        groups.append(refs[at:at + n])
        at += n
    return (*groups, refs[at:])


def _gain(gains_ref, row):
    return gains_ref[row:row + 1, :]


def _pre_attention_kernel(h_ref, gains_ref, *refs):
    (ffn, (wk_ref, wt_ref), (ws_ref, lng_ref, lnb_ref, bs_ref, gn_ref),
     (h1_ref, k_ref, qt_ref, vt_ref, gmlp_ref, a_ref, vn_ref, gated_ref)) = _split(
        refs, N_FFN_REFS, N_INPROJ_REFS, N_GMLP_REFS)
    t_idx = lax.broadcasted_iota(jnp.int32, (CHUNK, CHUNK), 0)
    s_idx = lax.broadcasted_iota(jnp.int32, (CHUNK, CHUNK), 1)
    w_causal = jnp.where(s_idx <= t_idx, ws_ref[...], 0.0).astype(BF16)

    def project_gates(r, h1):
        h1_ref[SUB_TILES[r], :] = h1
        hn = _rms(h1, _gain(gains_ref, G_MIX_PRE)).astype(BF16)
        gu, gv = _inproj_gates(hn, wt_ref)
        _gmlp_norm(gv, lng_ref, lnb_ref, vn_ref.at[r])
        return hn, gu

    def mix(r, gu):
        _gmlp_mix(gu, SUB_TILES[r], w_causal, bs_ref, gn_ref, gmlp_ref, vn_ref.at[r], gated_ref.at[r])

    def project_qkv(r, hn):
        _inproj_qkv(hn, SUB_TILES[r], wk_ref, wt_ref, k_ref, qt_ref, vt_ref)

    _, h1_second, (hn0, gu0) = _ffn_pair(
        [h_ref[rows, :] for rows in SUB_TILES], _gain(gains_ref, G_FFN1_PRE), _gain(gains_ref, G_FFN1_POST),
        *ffn, a_ref, before_second_down=functools.partial(project_gates, 0))
    project_qkv(0, hn0)
    hn1, gu1 = project_gates(1, h1_second)
    mix(0, gu0)
    project_qkv(1, hn1)
    mix(1, gu1)


def _pre_attention(h, gains, ffn, inproj, gmlp):
    tiles_per_seq = SEQ // TM
    row = pl.BlockSpec((TM, D_MODEL), lambda i: (i, 0))
    tok = pl.BlockSpec((TM, ATTN_W), lambda i: (i, 0))
    feat = pl.BlockSpec((1, ATTN_W, TM), lambda i: (i // tiles_per_seq, 0, i % tiles_per_seq))
    tok_shape = jax.ShapeDtypeStruct((TOKENS, ATTN_W), BF16)
    feat_shape = jax.ShapeDtypeStruct((BATCH, ATTN_W, SEQ), BF16)
    assert len(ffn) == N_FFN_REFS and len(inproj) == N_INPROJ_REFS and len(gmlp) == N_GMLP_REFS
    operands, specs = zip(gains, *ffn, *inproj, *gmlp)
    return pl.pallas_call(
        _pre_attention_kernel,
        grid=(TOKENS // TM,),
        in_specs=[row, *specs],
        out_specs=[row, tok, feat, feat, tok],
        out_shape=[jax.ShapeDtypeStruct((TOKENS, D_MODEL), F32), tok_shape, feat_shape, feat_shape,
                   jax.ShapeDtypeStruct((TOKENS, GMLP_W), BF16)],
        scratch_shapes=[pltpu.VMEM((TM // SUB, SUB, D_FF), BF16),
                        pltpu.VMEM((TM // SUB, N_GROUPS, (SUB // CHUNK) * HEAD_DIM, CHUNK), BF16),
                        pltpu.VMEM((TM // SUB, SUB // CHUNK, GMLP_W, CHUNK), F32)],
        compiler_params=_params(1),
        name="ffn1_inproj_gmlp",
    )(h, *operands)


def _post_attention_kernel(h_ref, attn_ref, gmlp_ref, p_ref, gains_ref, *refs):
    outproj, ffn, ple, (o_ref, a_ref) = _split(refs, N_OUTPROJ_REFS, N_FFN_REFS, N_PLE_REFS)
    hs = [_outproj_stage(h_ref[rows, :], rows, attn_ref, gmlp_ref, _gain(gains_ref, G_MIX_POST), *outproj)
          for rows in SUB_TILES]
    wg_ref, wp_ref = ple

    def project_p(_):
        return [_dot(p_ref[rows, :].astype(BF16), wp_ref[...]) for rows in SUB_TILES]

    *hs, p_proj = _ffn_pair(hs, _gain(gains_ref, G_FFN2_PRE), _gain(gains_ref, G_FFN2_POST), *ffn, a_ref,
                            before_second_down=project_p)
    for rows, h, pp in zip(SUB_TILES, hs, p_proj):
        o_ref[rows, :] = _ple_stage(h, pp, _gain(gains_ref, G_PLE_PRE), _gain(gains_ref, G_PLE_POST), wg_ref)


def _post_attention(h, attn, gmlp, p, layer, gains, outproj, ffn, ple):
    row = pl.BlockSpec((TM, D_MODEL), lambda i: (i, 0))
    tok = pl.BlockSpec((TM, ATTN_W), lambda i: (i, 0))
    assert len(outproj) == N_OUTPROJ_REFS and len(ffn) == N_FFN_REFS and len(ple) == N_PLE_REFS
    operands, specs = zip(gains, *outproj, *ffn, *ple)
    return pl.pallas_call(
        _post_attention_kernel,
        grid=(TOKENS // TM,),
        in_specs=[row, tok, tok, pl.BlockSpec((None, TM, PLE_DIM), lambda i: (layer, i, 0)), *specs],
        out_specs=row,
        out_shape=jax.ShapeDtypeStruct((TOKENS, D_MODEL), F32),
        scratch_shapes=[pltpu.VMEM((TM // SUB, SUB, D_FF), BF16)],
        compiler_params=_params(1),
        name="outproj_ffn2_ple",
    )(h, attn, gmlp, p, *operands)


SPLIT = 4
POS_ROWS = 4 * SPLIT
CHOICE_ROWS = N_BLK
AUX_CHOICE = POS_ROWS
SCORE_LOOKAHEAD = 4
PV_DELAY = 1
SUM_ROWS = 16
LOG2E = float(np.log2(np.e))


def _split_bf16(x):
    terms, rest = [], x
    for _ in range(SPLIT):
        t = rest.astype(BF16).astype(np.float64)
        terms.append(t)
        rest = rest - t
    assert np.max(np.abs(rest)) <= 1e-7 * max(1.0, np.max(np.abs(x)))
    return terms


def _attention_constants():
    start = 2.0 ** (-8.0 / N_HEADS)
    slopes = np.array([start ** (i + 1) for i in range(N_HEADS)], dtype=np.float32).astype(np.float64)
    pos = np.arange(SEQ)
    local = (pos % BLK).astype(np.float64)
    blk = pos // BLK
    kaux = np.zeros((N_HEADS, SEQ, PAIR_W), np.float64)
    qaux = np.zeros((N_HEADS, POS_ROWS, SEQ), np.float64)
    for h in range(N_HEADS):
        base = (1 - h % 2) * HEAD_DIM
        c = LOG2E * slopes[h]
        fine, coarse = _split_bf16(c * local), _split_bf16(c * BLK * blk)
        for n in range(SPLIT):
            kaux[h, :, base + n] = 1.0
            qaux[h, n] = -fine[n]
            kaux[h, :, base + SPLIT + n] = fine[n]
            qaux[h, SPLIT + n] = 1.0
            kaux[h, :, base + 2 * SPLIT + n] = 1.0
            qaux[h, 2 * SPLIT + n] = -coarse[n]
            kaux[h, :, base + 3 * SPLIT + n] = coarse[n]
            qaux[h, 3 * SPLIT + n] = 1.0
        kaux[h, pos, base + AUX_CHOICE + blk] = 1.0
    assert np.array_equal(kaux.astype(BF16).astype(np.float64), kaux)
    assert np.array_equal(qaux.astype(BF16).astype(np.float64), qaux)
    return jnp.asarray(kaux.astype(BF16)), jnp.asarray(qaux.astype(np.float32))


def _attn_kernel(kaux_ref, qaux_ref, k_ref, qt_ref, vt_ref, o_ref, kaug_ref, choice_ref, vaug_ref):
    lane = lax.broadcasted_iota(jnp.int32, (1, PAIR_W), 1)
    k_all = k_ref[...]
    q_all = qt_ref[0]
    k_mean = jnp.mean(k_all.astype(F32).reshape(N_BLK, BLK, PAIR_W), axis=1)
    blk_of_row = lax.broadcasted_iota(jnp.int32, (N_BLK, SEQ), 0)
    blk_of_query = lax.broadcasted_iota(jnp.int32, (N_BLK, SEQ), 1) // BLK
    in_past = blk_of_row < blk_of_query
    for a in range(2):
        in_head = (lane >= a * HEAD_DIM) & (lane < (a + 1) * HEAD_DIM)
        kaug_ref[a] = jnp.where(in_head, k_all, kaux_ref[a])
        km = jnp.where(in_head, k_mean, 0.0)
        km_hi = km.astype(BF16)
        km_lo = (km - km_hi.astype(F32)).astype(BF16)
        gate = _dot(km_hi, q_all) + _dot(km_lo, q_all)
        gate = jnp.where(in_past, gate, NEG)
        rank = jnp.zeros((N_BLK, SEQ), jnp.int32)
        for other in range(N_BLK):
            row = gate[other:other + 1, :]
            beats = (row > gate) | ((row == gate) & (other < blk_of_row))
            rank = rank + beats.astype(jnp.int32)
        chosen = (rank < TOP_K) & (gate > 0.5 * NEG)
        choice_ref[a] = jnp.where(in_past & jnp.logical_not(chosen), NEG, 0.0)
        vaug_ref[a, :HEAD_DIM, :] = vt_ref[0, a * HEAD_DIM:(a + 1) * HEAD_DIM, :]
        vaug_ref[a, HEAD_DIM:, :] = jnp.ones((SUM_ROWS, SEQ), BF16)

    key_idx = lax.broadcasted_iota(jnp.int32, (BLK, BLK), 0)
    query_idx = lax.broadcasted_iota(jnp.int32, (BLK, BLK), 1)
    causal = key_idx <= query_idx
    aux_fill = jnp.zeros((HEAD_DIM - POS_ROWS - CHOICE_ROWS, BLK), F32)

    def blk_slice(j):
        return slice(j * BLK, (j + 1) * BLK)

    tiles = [(i, a, j) for i in range(N_BLK) for j in [i] + list(range(i)) for a in range(2)]
    q_augs, scores, probs, running_max, acc_state, outs = {}, {}, {}, {}, {}, {}

    def issue_scores(n):
        i, a, j = tiles[n]
        if (i, a) not in q_augs:
            feat = slice(a * HEAD_DIM, (a + 1) * HEAD_DIM)
            q_head = qt_ref[0, feat, blk_slice(i)].astype(F32) * (LOG2E * HEAD_DIM ** -0.5)
            aux = jnp.concatenate(
                [qaux_ref[a, :, blk_slice(i)], choice_ref[a, :, blk_slice(i)], aux_fill], axis=0)
            q_augs[i, a] = jnp.concatenate(
                [q_head, aux] if a == 0 else [aux, q_head], axis=0).astype(BF16)
        scores[n] = _dot(kaug_ref[a, blk_slice(j), :], q_augs[i, a])

    def softmax_tile(n):
        i, a, j = tiles[n]
        s = scores.pop(n)
        if j == i:
            s = jnp.where(causal, s, NEG)
            m = jnp.max(s, axis=0, keepdims=True)
            alpha = None
        else:
            m_old = running_max[i, a]
            m = jnp.maximum(m_old, jnp.max(s, axis=0, keepdims=True))
            alpha = jnp.exp2(m_old - m)
        running_max[i, a] = m
        probs[n] = (jnp.exp2(s - m).astype(BF16), alpha)

    def weighted_values(n):
        i, a, j = tiles[n]
        p, alpha = probs.pop(n)
        pv = _dot(vaug_ref[a, :, blk_slice(j)], p)
        acc = pv if alpha is None else alpha * acc_state[i, a] + pv
        acc_state[i, a] = acc
        if j == (i - 1 if i > 0 else 0):
            outs.setdefault(i, []).append(acc[:HEAD_DIM] / acc[HEAD_DIM:HEAD_DIM + 1])
            del acc_state[i, a]
            if a == 1:
                o_ref[blk_slice(i), :] = jnp.concatenate(outs.pop(i), axis=0).T.astype(BF16)

    for n in range(min(SCORE_LOOKAHEAD, len(tiles))):
        issue_scores(n)
    for n in range(len(tiles) + PV_DELAY):
        if n + SCORE_LOOKAHEAD < len(tiles):
            issue_scores(n + SCORE_LOOKAHEAD)
        if n < len(tiles):
            softmax_tile(n)
        if n >= PV_DELAY:
            weighted_values(n - PV_DELAY)


def _attention(kaux, qaux, k, qt, vt):
    tok = pl.BlockSpec((SEQ, PAIR_W), lambda b, p: (b, p))
    feat = pl.BlockSpec((1, PAIR_W, SEQ), lambda b, p: (b, p, 0))
    return pl.pallas_call(
        _attn_kernel,
        grid=(BATCH, ATTN_W // PAIR_W),
        in_specs=[pl.BlockSpec((2, SEQ, PAIR_W), lambda b, p: (p, 0, 0)),
                  pl.BlockSpec((2, POS_ROWS, SEQ), lambda b, p: (p, 0, 0)),
                  tok, feat, feat],
        out_specs=tok,
        out_shape=jax.ShapeDtypeStruct((TOKENS, ATTN_W), BF16),
        scratch_shapes=[pltpu.VMEM((2, SEQ, PAIR_W), BF16),
                        pltpu.VMEM((2, N_BLK, SEQ), F32),
                        pltpu.VMEM((2, HEAD_DIM + SUM_ROWS, SEQ), BF16)],
        compiler_params=_params(2),
        name="moba_attention",
    )(kaux, qaux, k, qt, vt)


CAST_ROWS = 256


def _cast_kernel(*refs):
    n = len(refs) // 2
    for src, dst in zip(refs[:n], refs[n:]):
        dst[...] = src[...].astype(BF16)


def _cast_bf16(*stacked):
    depth, rows, cols = stacked[0].shape
    assert all(w.shape == stacked[0].shape for w in stacked) and rows % CAST_ROWS == 0
    spec = pl.BlockSpec((None, CAST_ROWS, cols), lambda d, r: (d, r, 0))
    return pl.pallas_call(
        _cast_kernel,
        grid=(depth, rows // CAST_ROWS),
        in_specs=[spec] * len(stacked),
        out_specs=[spec] * len(stacked),
        out_shape=[jax.ShapeDtypeStruct(w.shape, BF16) for w in stacked],
        compiler_params=_params(2),
        name="cast_bf16",
    )(*stacked)


def _transpose_cast_kernel(w_ref, o_ref):
    o_ref[...] = w_ref[...].T.astype(BF16)


def _inproj_weights_transposed(w_in):
    return pl.pallas_call(
        _transpose_cast_kernel,
        grid=(DEPTH, 4),
        in_specs=[pl.BlockSpec((None, D_MODEL, ATTN_W), lambda d, r: (d, 0, r + jnp.minimum(r, 1)))],
        out_specs=pl.BlockSpec((None, ATTN_W, D_MODEL), lambda d, r: (d, r, 0)),
        out_shape=jax.ShapeDtypeStruct((DEPTH, 4 * ATTN_W, D_MODEL), BF16),
        compiler_params=_params(2),
        name="inproj_weights_t",
    )(w_in)


def kernel(x, p, ffn1_pre_norm, ffn1_w_gate, ffn1_w_up, ffn1_w_down, ffn1_post_norm, mix_pre_norm, w_in, gmlp_ln_g, gmlp_ln_b, gmlp_w_s, gmlp_b_s, attn_out_norm, gmlp_out_norm, w_out, mix_post_norm, ffn2_pre_norm, ffn2_w_gate, ffn2_w_up, ffn2_w_down, ffn2_post_norm, ple_pre_norm, ple_w_gate, ple_w_proj, ple_post_norm):
    assert x.shape == (BATCH, SEQ, D_MODEL) and p.shape == (DEPTH, BATCH, SEQ, PLE_DIM)
    assert ATTN_W == GMLP_W and w_in.shape == (DEPTH, D_MODEL, 5 * ATTN_W)
    kaux, qaux = _attention_constants()
    h = x.reshape(TOKENS, D_MODEL)
    p = p.reshape(DEPTH, TOKENS, PLE_DIM)

    def col(v):
        return jnp.broadcast_to(v[:, :, None], (*v.shape, CHUNK))

    gains = jnp.stack([ffn1_pre_norm, ffn1_post_norm, mix_pre_norm, mix_post_norm,
                       ffn2_pre_norm, ffn2_post_norm, ple_pre_norm, ple_post_norm], axis=1)
    assert gains.shape == (DEPTH, N_GAINS, D_MODEL)
    f1g, f1u, f2g, f2u = _cast_bf16(ffn1_w_gate, ffn1_w_up, ffn2_w_gate, ffn2_w_up)
    f1d, f2d = _cast_bf16(ffn1_w_down, ffn2_w_down)
    wo, pwg = _cast_bf16(w_out, ple_w_gate)
    w_t = _inproj_weights_transposed(w_in)
    w_k = w_in[:, :, ATTN_W:2 * ATTN_W].astype(BF16)
    pwp = ple_w_proj.astype(BF16)
    ln_g, ln_b, gn = col(gmlp_ln_g), col(gmlp_ln_b), col(gmlp_out_norm)
    attn_gain = attn_out_norm.reshape(DEPTH, 1, ATTN_W)

    for i in range(DEPTH):
        def layer(w, block=None, index=None):
            return _layer_operand(w, i, block, index)

        gmlp = (layer(gmlp_w_s), layer(ln_g), layer(ln_b), layer(gmlp_b_s), layer(gn))
        h, k, qt, vt, gmlp_n = _pre_attention(
            h, layer(gains), (layer(f1g), layer(f1u), layer(f1d)), (layer(w_k), layer(w_t)), gmlp)
        attn = _attention(kaux, qaux, k, qt, vt)
        outproj = (layer(attn_gain), layer(wo, (ATTN_W, D_MODEL), (0, 0)), layer(wo, (GMLP_W, D_MODEL), (1, 0)))
        h = _post_attention(h, attn, gmlp_n, p, i, layer(gains), outproj,
                            (layer(f2g), layer(f2u), layer(f2d)), (layer(pwg), layer(pwp)))
    return h.reshape(BATCH, SEQ, D_MODEL)
```

```python
import functools
from typing import NamedTuple

import numpy as np
import jax
import jax.numpy as jnp
from jax import lax
from jax.experimental import pallas as pl
from jax.experimental.pallas import tpu as pltpu

D_MODEL = 1024
BATCH = 16
SEQ = 2048
DEPTH = 2
HEAD_DIM = 64
N_HEADS = 8
N_GROUPS = 8
ATTN_W = N_HEADS * HEAD_DIM
GMLP_W = N_GROUPS * HEAD_DIM
BLK = 256
N_BLK = SEQ // BLK
TOP_K = 3
CHUNK = 128
D_FF = 2816
PLE_DIM = 256
RMS_EPS = 1e-6
LN_EPS = 1e-5
NEG = -1e30
TOKENS = BATCH * SEQ

LANES = 128
BF16_SUBLANES = 16
PAIR_W = 2 * HEAD_DIM
TM = 512
SUB = 256
FF_CHUNK = 256
VMEM_LIMIT = 56 * 1024 * 1024

F32 = jnp.float32
BF16 = jnp.bfloat16


def _rms(x, g):
    return x * lax.rsqrt(jnp.mean(x * x, axis=-1, keepdims=True) + RMS_EPS) * g


def _gelu(x):
    return 0.5 * x * (1.0 + lax.erf(x * np.float32(np.sqrt(0.5))))


def _dot(a, b):
    return jnp.dot(a, b, preferred_element_type=F32)


def _dot_nt(a, b):
    return lax.dot_general(a, b, (((1,), (1,)), ((), ())), preferred_element_type=F32)


def _layer_operand(stacked, layer):
    block = stacked.shape[1:]
    spec = pl.BlockSpec((None, *block), lambda *_: (layer, *(0,) * len(block)), pipeline_mode=pl.Buffered(1))
    return stacked, spec


def _whole(array, block=None, index=None):
    block = array.shape if block is None else block
    index = (0,) * len(block) if index is None else index
    return array, pl.BlockSpec(block, lambda *_: index, pipeline_mode=pl.Buffered(1))


def _params(n_axes):
    return pltpu.CompilerParams(
        dimension_semantics=("arbitrary",) * n_axes, vmem_limit_bytes=VMEM_LIMIT)


def _ffn_up(hn, chunks, wg_ref, wu_ref, a_ref):
    for c in chunks:
        sl = slice(c * FF_CHUNK, (c + 1) * FF_CHUNK)
        g = _dot(hn, wg_ref[:, sl])
        u = _dot(hn, wu_ref[:, sl])
        a_ref[:, sl] = (g * jax.nn.sigmoid(g) * u).astype(BF16)


def _ffn_pair(hs, gpre, gpost, wg_ref, wu_ref, wd_ref, a_ref, before_second_down):
    n_chunks = D_FF // FF_CHUNK
    hn0, hn1 = [_rms(h, gpre).astype(BF16) for h in hs]
    _ffn_up(hn0, range(n_chunks), wg_ref, wu_ref, a_ref.at[0])
    _ffn_up(hn1, range(FFN_AHEAD), wg_ref, wu_ref, a_ref.at[1])
    out0 = hs[0] + 0.5 * _rms(_dot(a_ref[0], wd_ref[...]), gpost)
    _ffn_up(hn1, range(FFN_AHEAD, n_chunks), wg_ref, wu_ref, a_ref.at[1])
    filler = before_second_down(out0)
    out1 = hs[1] + 0.5 * _rms(_dot(a_ref[1], wd_ref[...]), gpost)
    return out0, out1, filler


def _feature_major(wt_ref, r, hn):
    return _dot_nt(wt_ref[r * ATTN_W:(r + 1) * ATTN_W, :], hn)


def _inproj_gates(hn, wt_ref):
    gv = _gelu(_feature_major(wt_ref, 3, hn))
    return _gelu(_feature_major(wt_ref, 2, hn)), gv


def _inproj_qkv(hn, rows, wk_ref, wt_ref, k_ref, qt_ref, vt_ref):
    k_ref[rows, :] = _dot(hn, wk_ref[...]).astype(BF16)
    qt_ref[0, :, rows] = _feature_major(wt_ref, 0, hn).astype(BF16)
    vt_ref[0, :, rows] = _feature_major(wt_ref, 1, hn).astype(BF16)


def _gmlp_norm(gv, lng_ref, lnb_ref, vn_ref):
    for c in range(SUB // CHUNK):
        v = gv[:, c * CHUNK:(c + 1) * CHUNK].reshape(N_GROUPS, HEAD_DIM, CHUNK)
        mu = jnp.mean(v, axis=1, keepdims=True)
        var = jnp.mean(jnp.square(v - mu), axis=1, keepdims=True)
        vn = ((v - mu) * lax.rsqrt(var + LN_EPS)).reshape(GMLP_W, CHUNK)
        vn = (vn * lng_ref[...] + lnb_ref[...]).astype(BF16).reshape(N_GROUPS, HEAD_DIM, CHUNK)
        vn_ref[:, c * HEAD_DIM:(c + 1) * HEAD_DIM, :] = vn


def _gmlp_mix(gu, rows, w_causal, bs_ref, gn_ref, o_ref, vn_ref, gated_ref):
    n_chunks = SUB // CHUNK
    ssq = jnp.zeros((n_chunks, 1, CHUNK), F32)
    for g in range(N_GROUPS):
        mixed = _dot_nt(vn_ref[g], w_causal[g]) + bs_ref[g:g + 1, :]
        feat = slice(g * HEAD_DIM, (g + 1) * HEAD_DIM)
        u = jnp.concatenate([gu[feat, c * CHUNK:(c + 1) * CHUNK] for c in range(n_chunks)], axis=0)
        gated = (u * mixed).reshape(n_chunks, HEAD_DIM, CHUNK)
        ssq = ssq + jnp.sum(gated * gated, axis=1, keepdims=True)
        gated_ref[:, feat, :] = gated
    inv = lax.rsqrt(ssq * (1.0 / GMLP_W) + RMS_EPS)
    o_ref[rows, :] = jnp.concatenate(
        [(gated_ref[c] * inv[c] * gn_ref[...]).T for c in range(n_chunks)], axis=0).astype(BF16)


def _outproj_stage(h, rows, attn_ref, gmlp_ref, gpost, an_ref, woa_ref, wog_ref):
    attn_n = _rms(attn_ref[rows, :].astype(F32), an_ref[...]).astype(BF16)
    m = _dot(attn_n, woa_ref[...]) + _dot(gmlp_ref[rows, :], wog_ref[...])
    return h + _rms(m, gpost)


def _ple_stage(h, p_proj, gpre, gpost, wg_ref):
    hn = _rms(h, gpre).astype(BF16)
    gate = jax.nn.sigmoid(_dot(hn, wg_ref[...]))
    return h + _rms(gate * p_proj, gpost)


(G_FFN1_PRE, G_FFN1_POST, G_MIX_PRE, G_MIX_POST, G_FFN2_PRE, G_FFN2_POST, G_PLE_PRE, G_PLE_POST) = range(8)
N_GAINS = 8
N_FFN_REFS = 3
N_INPROJ_REFS = 2
N_GMLP_REFS = 5
N_OUTPROJ_REFS = 3
N_PLE_REFS = 2
SUB_TILES = [slice(r * SUB, (r + 1) * SUB) for r in range(TM // SUB)]
assert len(SUB_TILES) == 2
FFN_AHEAD = 2


def _split(refs, *sizes):
    groups, at = [], 0
    for n in sizes:
        groups.append(refs[at:at + n])
        at += n
    return (*groups, refs[at:])


def _gain(gains_ref, row):
    return gains_ref[row:row + 1, :]


class _SideCast(NamedTuple):
    source: jax.Array
    in_spec: pl.BlockSpec
    out_shape: jax.ShapeDtypeStruct
    out_spec: pl.BlockSpec
    transpose: bool


def _side_cast(stacked, layer, cols=None, col_block=0):
    _, n_rows, n_cols = stacked.shape
    cols = n_cols if cols is None else cols
    steps = max(s for s in range(1, TOKENS // TM + 1)
                if n_rows % s == 0 and (n_rows // s) % BF16_SUBLANES == 0)
    rows = n_rows // steps
    return _SideCast(
        stacked,
        pl.BlockSpec((None, rows, cols), lambda i: (layer, jnp.minimum(i, steps - 1), col_block)),
        jax.ShapeDtypeStruct((n_rows, cols), BF16),
        pl.BlockSpec((rows, cols), lambda i: (jnp.minimum(i, steps - 1), 0)),
        False)


def _side_cast_inproj_transposed(w_in, layer):
    k_first, k_blocks = ATTN_W // LANES, ATTN_W // LANES
    col_blocks = 4 * ATTN_W // LANES
    row_blocks = (TOKENS // TM) // col_blocks
    rows = D_MODEL // row_blocks
    assert row_blocks >= 1 and rows % LANES == 0

    def tile(i):
        c = jnp.minimum(i // row_blocks, col_blocks - 1)
        return i % row_blocks, c, c + jnp.where(c >= k_first, k_blocks, 0)

    return _SideCast(
        w_in,
        pl.BlockSpec((None, rows, LANES), lambda i: (layer, tile(i)[0], tile(i)[2])),
        jax.ShapeDtypeStruct((4 * ATTN_W, D_MODEL), BF16),
        pl.BlockSpec((LANES, rows), lambda i: (tile(i)[1], tile(i)[0])),
        True)


def _run_side_casts(transposes, src_refs, dst_refs):
    for transpose, src, dst in zip(transposes, src_refs, dst_refs):
        x = src[...]
        dst[...] = (x.T if transpose else x).astype(BF16)


def _pre_attention_kernel(side, h_ref, gains_ref, *refs):
    (ffn, (wk_ref, wt_ref), (ws_ref, lng_ref, lnb_ref, bs_ref, gn_ref), side_in,
     (h1_ref, k_ref, qt_ref, vt_ref, gmlp_ref), side_out, (a_ref, vn_ref, gated_ref)) = _split(
        refs, N_FFN_REFS, N_INPROJ_REFS, N_GMLP_REFS, len(side), 5, len(side))
    t_idx = lax.broadcasted_iota(jnp.int32, (CHUNK, CHUNK), 0)
    s_idx = lax.broadcasted_iota(jnp.int32, (CHUNK, CHUNK), 1)
    w_causal = jnp.where(s_idx <= t_idx, ws_ref[...], 0.0).astype(BF16)

    def project_gates(r, h1):
        h1_ref[SUB_TILES[r], :] = h1
        hn = _rms(h1, _gain(gains_ref, G_MIX_PRE)).astype(BF16)
        gu, gv = _inproj_gates(hn, wt_ref)
        _gmlp_norm(gv, lng_ref, lnb_ref, vn_ref.at[r])
        return hn, gu

    def mix(r, gu):
        _gmlp_mix(gu, SUB_TILES[r], w_causal, bs_ref, gn_ref, gmlp_ref, vn_ref.at[r], gated_ref.at[r])

    def project_qkv(r, hn):
        _inproj_qkv(hn, SUB_TILES[r], wk_ref, wt_ref, k_ref, qt_ref, vt_ref)

    _, h1_second, (hn0, gu0) = _ffn_pair(
        [h_ref[rows, :] for rows in SUB_TILES], _gain(gains_ref, G_FFN1_PRE), _gain(gains_ref, G_FFN1_POST),
        *ffn, a_ref, before_second_down=functools.partial(project_gates, 0))
    _run_side_casts(side, side_in, side_out)
    project_qkv(0, hn0)
    hn1, gu1 = project_gates(1, h1_second)
    mix(0, gu0)
    project_qkv(1, hn1)
    mix(1, gu1)


def _pre_attention(h, gains, ffn, inproj, gmlp, side):
    tiles_per_seq = SEQ // TM
    row = pl.BlockSpec((TM, D_MODEL), lambda i: (i, 0))
    tok = pl.BlockSpec((TM, ATTN_W), lambda i: (i, 0))
    feat = pl.BlockSpec((1, ATTN_W, TM), lambda i: (i // tiles_per_seq, 0, i % tiles_per_seq))
    tok_shape = jax.ShapeDtypeStruct((TOKENS, ATTN_W), BF16)
    feat_shape = jax.ShapeDtypeStruct((BATCH, ATTN_W, SEQ), BF16)
    assert len(ffn) == N_FFN_REFS and len(inproj) == N_INPROJ_REFS and len(gmlp) == N_GMLP_REFS
    operands, specs = zip(gains, *ffn, *inproj, *gmlp)
    return pl.pallas_call(
        functools.partial(_pre_attention_kernel, [c.transpose for c in side]),
        grid=(TOKENS // TM,),
        in_specs=[row, *specs, *[c.in_spec for c in side]],
        out_specs=[row, tok, feat, feat, tok, *[c.out_spec for c in side]],
        out_shape=[jax.ShapeDtypeStruct((TOKENS, D_MODEL), F32), tok_shape, feat_shape, feat_shape,
                   jax.ShapeDtypeStruct((TOKENS, GMLP_W), BF16), *[c.out_shape for c in side]],
        scratch_shapes=[pltpu.VMEM((TM // SUB, SUB, D_FF), BF16),
                        pltpu.VMEM((TM // SUB, N_GROUPS, (SUB // CHUNK) * HEAD_DIM, CHUNK), BF16),
                        pltpu.VMEM((TM // SUB, SUB // CHUNK, GMLP_W, CHUNK), F32)],
        compiler_params=_params(1),
        name="ffn1_inproj_gmlp",
    )(h, *operands, *[c.source for c in side])


def _post_attention_kernel(side, h_ref, attn_ref, gmlp_ref, p_ref, gains_ref, *refs):
    outproj, ffn, ple, side_in, (o_ref,), side_out, (a_ref,) = _split(
        refs, N_OUTPROJ_REFS, N_FFN_REFS, N_PLE_REFS, len(side), 1, len(side))
    hs = [_outproj_stage(h_ref[rows, :], rows, attn_ref, gmlp_ref, _gain(gains_ref, G_MIX_POST), *outproj)
          for rows in SUB_TILES]
    wg_ref, wp_ref = ple

    def project_p(_):
        return [_dot(p_ref[rows, :].astype(BF16), wp_ref[...]) for rows in SUB_TILES]

    *hs, p_proj = _ffn_pair(hs, _gain(gains_ref, G_FFN2_PRE), _gain(gains_ref, G_FFN2_POST), *ffn, a_ref,
                            before_second_down=project_p)
    _run_side_casts(side, side_in, side_out)
    for rows, h, pp in zip(SUB_TILES, hs, p_proj):
        o_ref[rows, :] = _ple_stage(h, pp, _gain(gains_ref, G_PLE_PRE), _gain(gains_ref, G_PLE_POST), wg_ref)


def _post_attention(h, attn, gmlp, p, layer, gains, outproj, ffn, ple, side):
    row = pl.BlockSpec((TM, D_MODEL), lambda i: (i, 0))
    tok = pl.BlockSpec((TM, ATTN_W), lambda i: (i, 0))
    assert len(outproj) == N_OUTPROJ_REFS and len(ffn) == N_FFN_REFS and len(ple) == N_PLE_REFS
    operands, specs = zip(gains, *outproj, *ffn, *ple)
    return pl.pallas_call(
        functools.partial(_post_attention_kernel, [c.transpose for c in side]),
        grid=(TOKENS // TM,),
        in_specs=[row, tok, tok, pl.BlockSpec((None, TM, PLE_DIM), lambda i: (layer, i, 0)), *specs,
                  *[c.in_spec for c in side]],
        out_specs=[row, *[c.out_spec for c in side]],
        out_shape=[jax.ShapeDtypeStruct((TOKENS, D_MODEL), F32), *[c.out_shape for c in side]],
        scratch_shapes=[pltpu.VMEM((TM // SUB, SUB, D_FF), BF16)],
        compiler_params=_params(1),
        name="outproj_ffn2_ple",
    )(h, attn, gmlp, p, *operands, *[c.source for c in side])


SPLIT = 4
POS_ROWS = 4 * SPLIT
CHOICE_ROWS = N_BLK
AUX_CHOICE = POS_ROWS
SCORE_LOOKAHEAD = 4
PV_DELAY = 1
SUM_ROWS = 16
LOG2E = float(np.log2(np.e))


def _split_bf16(x):
    terms, rest = [], x
    for _ in range(SPLIT):
        t = rest.astype(BF16).astype(np.float64)
        terms.append(t)
        rest = rest - t
    assert np.max(np.abs(rest)) <= 1e-7 * max(1.0, np.max(np.abs(x)))
    return terms


def _attention_constants():
    start = 2.0 ** (-8.0 / N_HEADS)
    slopes = np.array([start ** (i + 1) for i in range(N_HEADS)], dtype=np.float32).astype(np.float64)
    pos = np.arange(SEQ)
    local = (pos % BLK).astype(np.float64)
    blk = pos // BLK
    kaux = np.zeros((N_HEADS, SEQ, PAIR_W), np.float64)
    qaux = np.zeros((N_HEADS, POS_ROWS, SEQ), np.float64)
    for h in range(N_HEADS):
        base = (1 - h % 2) * HEAD_DIM
        c = LOG2E * slopes[h]
        fine, coarse = _split_bf16(c * local), _split_bf16(c * BLK * blk)
        for n in range(SPLIT):
            kaux[h, :, base + n] = 1.0
            qaux[h, n] = -fine[n]
            kaux[h, :, base + SPLIT + n] = fine[n]
            qaux[h, SPLIT + n] = 1.0
            kaux[h, :, base + 2 * SPLIT + n] = 1.0
            qaux[h, 2 * SPLIT + n] = -coarse[n]
            kaux[h, :, base + 3 * SPLIT + n] = coarse[n]
            qaux[h, 3 * SPLIT + n] = 1.0
        kaux[h, pos, base + AUX_CHOICE + blk] = 1.0
    assert np.array_equal(kaux.astype(BF16).astype(np.float64), kaux)
    assert np.array_equal(qaux.astype(BF16).astype(np.float64), qaux)
    return jnp.asarray(kaux.astype(BF16)), jnp.asarray(qaux.astype(np.float32))


def _attn_kernel(kaux_ref, qaux_ref, k_ref, qt_ref, vt_ref, o_ref, kaug_ref, choice_ref, vaug_ref):
    lane = lax.broadcasted_iota(jnp.int32, (1, PAIR_W), 1)
    k_all = k_ref[...]
    q_all = qt_ref[0]
    k_mean = jnp.mean(k_all.astype(F32).reshape(N_BLK, BLK, PAIR_W), axis=1)
    blk_of_row = lax.broadcasted_iota(jnp.int32, (N_BLK, SEQ), 0)
    blk_of_query = lax.broadcasted_iota(jnp.int32, (N_BLK, SEQ), 1) // BLK
    in_past = blk_of_row < blk_of_query
    for a in range(2):
        in_head = (lane >= a * HEAD_DIM) & (lane < (a + 1) * HEAD_DIM)
        kaug_ref[a] = jnp.where(in_head, k_all, kaux_ref[a])
        km = jnp.where(in_head, k_mean, 0.0)
        km_hi = km.astype(BF16)
        km_lo = (km - km_hi.astype(F32)).astype(BF16)
        gate = _dot(km_hi, q_all) + _dot(km_lo, q_all)
        gate = jnp.where(in_past, gate, NEG)
        rank = jnp.zeros((N_BLK, SEQ), jnp.int32)
        for other in range(N_BLK):
            row = gate[other:other + 1, :]
            beats = (row > gate) | ((row == gate) & (other < blk_of_row))
            rank = rank + beats.astype(jnp.int32)
        chosen = (rank < TOP_K) & (gate > 0.5 * NEG)
        choice_ref[a] = jnp.where(in_past & jnp.logical_not(chosen), NEG, 0.0)
        vaug_ref[a, :HEAD_DIM, :] = vt_ref[0, a * HEAD_DIM:(a + 1) * HEAD_DIM, :]
        vaug_ref[a, HEAD_DIM:, :] = jnp.ones((SUM_ROWS, SEQ), BF16)

    key_idx = lax.broadcasted_iota(jnp.int32, (BLK, BLK), 0)
    query_idx = lax.broadcasted_iota(jnp.int32, (BLK, BLK), 1)
    causal = key_idx <= query_idx
    aux_fill = jnp.zeros((HEAD_DIM - POS_ROWS - CHOICE_ROWS, BLK), F32)

    def blk_slice(j):
        return slice(j * BLK, (j + 1) * BLK)

    tiles = [(i, a, j) for i in range(N_BLK) for j in [i] + list(range(i)) for a in range(2)]
    q_augs, scores, probs, running_max, acc_state, outs = {}, {}, {}, {}, {}, {}

    def issue_scores(n):
        i, a, j = tiles[n]
        if (i, a) not in q_augs:
            feat = slice(a * HEAD_DIM, (a + 1) * HEAD_DIM)
            q_head = qt_ref[0, feat, blk_slice(i)].astype(F32) * (LOG2E * HEAD_DIM ** -0.5)
            aux = jnp.concatenate(
                [qaux_ref[a, :, blk_slice(i)], choice_ref[a, :, blk_slice(i)], aux_fill], axis=0)
            q_augs[i, a] = jnp.concatenate(
                [q_head, aux] if a == 0 else [aux, q_head], axis=0).astype(BF16)
        scores[n] = _dot(kaug_ref[a, blk_slice(j), :], q_augs[i, a])

    def softmax_tile(n):
        i, a, j = tiles[n]
        s = scores.pop(n)
        if j == i:
            s = jnp.where(causal, s, NEG)
            m = jnp.max(s, axis=0, keepdims=True)
            alpha = None
        else:
            m_old = running_max[i, a]
            m = jnp.maximum(m_old, jnp.max(s, axis=0, keepdims=True))
            alpha = jnp.exp2(m_old - m)
        running_max[i, a] = m
        probs[n] = (jnp.exp2(s - m).astype(BF16), alpha)

    def weighted_values(n):
        i, a, j = tiles[n]
        p, alpha = probs.pop(n)
        pv = _dot(vaug_ref[a, :, blk_slice(j)], p)
        acc = pv if alpha is None else alpha * acc_state[i, a] + pv
        acc_state[i, a] = acc
        if j == (i - 1 if i > 0 else 0):
            outs.setdefault(i, []).append(acc[:HEAD_DIM] / acc[HEAD_DIM:HEAD_DIM + 1])
            del acc_state[i, a]
            if a == 1:
                o_ref[blk_slice(i), :] = jnp.concatenate(outs.pop(i), axis=0).T.astype(BF16)

    for n in range(min(SCORE_LOOKAHEAD, len(tiles))):
        issue_scores(n)
    for n in range(len(tiles) + PV_DELAY):
        if n + SCORE_LOOKAHEAD < len(tiles):
            issue_scores(n + SCORE_LOOKAHEAD)
        if n < len(tiles):
            softmax_tile(n)
        if n >= PV_DELAY:
            weighted_values(n - PV_DELAY)


def _attention(kaux, qaux, k, qt, vt):
    tok = pl.BlockSpec((SEQ, PAIR_W), lambda b, p: (b, p))
    feat = pl.BlockSpec((1, PAIR_W, SEQ), lambda b, p: (b, p, 0))
    return pl.pallas_call(
        _attn_kernel,
        grid=(BATCH, ATTN_W // PAIR_W),
        in_specs=[pl.BlockSpec((2, SEQ, PAIR_W), lambda b, p: (p, 0, 0)),
                  pl.BlockSpec((2, POS_ROWS, SEQ), lambda b, p: (p, 0, 0)),
                  tok, feat, feat],
        out_specs=tok,
        out_shape=jax.ShapeDtypeStruct((TOKENS, ATTN_W), BF16),
        scratch_shapes=[pltpu.VMEM((2, SEQ, PAIR_W), BF16),
                        pltpu.VMEM((2, N_BLK, SEQ), F32),
                        pltpu.VMEM((2, HEAD_DIM + SUM_ROWS, SEQ), BF16)],
        compiler_params=_params(2),
        name="moba_attention",
    )(kaux, qaux, k, qt, vt)


CAST_ROWS = 256


def _cast_kernel(*refs):
    n = len(refs) // 2
    for src, dst in zip(refs[:n], refs[n:]):
        dst[...] = src[...].astype(BF16)


def _cast_bf16(layer, *stacked):
    _, rows, cols = stacked[0].shape
    assert all(w.shape == stacked[0].shape for w in stacked) and rows % CAST_ROWS == 0
    return pl.pallas_call(
        _cast_kernel,
        grid=(rows // CAST_ROWS,),
        in_specs=[pl.BlockSpec((None, CAST_ROWS, cols), lambda r: (layer, r, 0))] * len(stacked),
        out_specs=[pl.BlockSpec((CAST_ROWS, cols), lambda r: (r, 0))] * len(stacked),
        out_shape=[jax.ShapeDtypeStruct((rows, cols), BF16) for _ in stacked],
        compiler_params=_params(1),
        name="cast_bf16",
    )(*stacked)


def _transpose_cast_kernel(w_ref, o_ref):
    o_ref[...] = w_ref[...].T.astype(BF16)


def _inproj_weights_transposed(w_in, layer):
    return pl.pallas_call(
        _transpose_cast_kernel,
        grid=(4,),
        in_specs=[pl.BlockSpec((None, D_MODEL, ATTN_W), lambda r: (layer, 0, r + jnp.minimum(r, 1)))],
        out_specs=pl.BlockSpec((ATTN_W, D_MODEL), lambda r: (r, 0)),
        out_shape=jax.ShapeDtypeStruct((4 * ATTN_W, D_MODEL), BF16),
        compiler_params=_params(1),
        name="inproj_weights_t",
    )(w_in)


def kernel(x, p, ffn1_pre_norm, ffn1_w_gate, ffn1_w_up, ffn1_w_down, ffn1_post_norm, mix_pre_norm, w_in, gmlp_ln_g, gmlp_ln_b, gmlp_w_s, gmlp_b_s, attn_out_norm, gmlp_out_norm, w_out, mix_post_norm, ffn2_pre_norm, ffn2_w_gate, ffn2_w_up, ffn2_w_down, ffn2_post_norm, ple_pre_norm, ple_w_gate, ple_w_proj, ple_post_norm):
    assert x.shape == (BATCH, SEQ, D_MODEL) and p.shape == (DEPTH, BATCH, SEQ, PLE_DIM)
    assert ATTN_W == GMLP_W and w_in.shape == (DEPTH, D_MODEL, 5 * ATTN_W)
    kaux, qaux = _attention_constants()
    h = x.reshape(TOKENS, D_MODEL)
    p = p.reshape(DEPTH, TOKENS, PLE_DIM)

    def col(v):
        return jnp.broadcast_to(v[:, :, None], (*v.shape, CHUNK))

    gains = jnp.stack([ffn1_pre_norm, ffn1_post_norm, mix_pre_norm, mix_post_norm,
                       ffn2_pre_norm, ffn2_post_norm, ple_pre_norm, ple_post_norm], axis=1)
    assert gains.shape == (DEPTH, N_GAINS, D_MODEL)
    ln_g, ln_b, gn = col(gmlp_ln_g), col(gmlp_ln_b), col(gmlp_out_norm)
    attn_gain = attn_out_norm.reshape(DEPTH, 1, ATTN_W)

    f1g, f1u = _cast_bf16(0, ffn1_w_gate, ffn1_w_up)
    (f1d,) = _cast_bf16(0, ffn1_w_down)
    w_t = _inproj_weights_transposed(w_in, 0)
    w_k = w_in[0, :, ATTN_W:2 * ATTN_W].astype(BF16)

    for i in range(DEPTH):
        def layer(w):
            return _layer_operand(w, i)

        post_weights = [_side_cast(w, i) for w in (ffn2_w_gate, ffn2_w_up, ffn2_w_down, w_out, ple_w_gate, ple_w_proj)]
        gmlp = (layer(gmlp_w_s), layer(ln_g), layer(ln_b), layer(gmlp_b_s), layer(gn))
        h, k, qt, vt, gmlp_n, f2g, f2u, f2d, wo, pwg, pwp = _pre_attention(
            h, layer(gains), (_whole(f1g), _whole(f1u), _whole(f1d)), (_whole(w_k), _whole(w_t)), gmlp,
            post_weights)
        attn = _attention(kaux, qaux, k, qt, vt)
        next_weights = [] if i + 1 == DEPTH else [
            _side_cast(ffn1_w_gate, i + 1), _side_cast(ffn1_w_up, i + 1), _side_cast(ffn1_w_down, i + 1),
            _side_cast(w_in, i + 1, cols=ATTN_W, col_block=1), _side_cast_inproj_transposed(w_in, i + 1)]
        outproj = (layer(attn_gain), _whole(wo, (ATTN_W, D_MODEL), (0, 0)), _whole(wo, (GMLP_W, D_MODEL), (1, 0)))
        h, *next_bf16 = _post_attention(h, attn, gmlp_n, p, i, layer(gains), outproj,
                                        (_whole(f2g), _whole(f2u), _whole(f2d)), (_whole(pwg), _whole(pwp)),
                                        next_weights)
        if next_bf16:
            f1g, f1u, f1d, w_k, w_t = next_bf16
    return h.reshape(BATCH, SEQ, D_MODEL)
```

```python
import functools
from typing import NamedTuple

import numpy as np
import jax
import jax.numpy as jnp
from jax import lax
from jax.experimental import pallas as pl
from jax.experimental.pallas import tpu as pltpu

D_MODEL = 1024
BATCH = 16
SEQ = 2048
DEPTH = 2
HEAD_DIM = 64
N_HEADS = 8
N_GROUPS = 8
ATTN_W = N_HEADS * HEAD_DIM
GMLP_W = N_GROUPS * HEAD_DIM
BLK = 256
N_BLK = SEQ // BLK
TOP_K = 3
CHUNK = 128
D_FF = 2816
PLE_DIM = 256
RMS_EPS = 1e-6
LN_EPS = 1e-5
NEG = -1e30
TOKENS = BATCH * SEQ

LANES = 128
BF16_SUBLANES = 16
PAIR_W = 2 * HEAD_DIM
TM = 512
SUB = 256
FF_CHUNK = 256
VMEM_LIMIT = 56 * 1024 * 1024

F32 = jnp.float32
BF16 = jnp.bfloat16


def _rms(x, g):
    return x * lax.rsqrt(jnp.mean(x * x, axis=-1, keepdims=True) + RMS_EPS) * g


def _gelu(x):
    return 0.5 * x * (1.0 + lax.erf(x * np.float32(np.sqrt(0.5))))


def _dot(a, b):
    return jnp.dot(a, b, preferred_element_type=F32)


def _dot_nt(a, b):
    return lax.dot_general(a, b, (((1,), (1,)), ((), ())), preferred_element_type=F32)


def _layer_operand(stacked, layer):
    block = stacked.shape[1:]
    spec = pl.BlockSpec((None, *block), lambda *_: (layer, *(0,) * len(block)), pipeline_mode=pl.Buffered(1))
    return stacked, spec


def _whole(array, block=None, index=None):
    block = array.shape if block is None else block
    index = (0,) * len(block) if index is None else index
    return array, pl.BlockSpec(block, lambda *_: index, pipeline_mode=pl.Buffered(1))


def _params(n_axes):
    return pltpu.CompilerParams(
        dimension_semantics=("arbitrary",) * n_axes, vmem_limit_bytes=VMEM_LIMIT)


def _ffn_up(hn, chunks, wg_ref, wu_ref, a_ref):
    for c in chunks:
        sl = slice(c * FF_CHUNK, (c + 1) * FF_CHUNK)
        g = _dot(hn, wg_ref[:, sl])
        u = _dot(hn, wu_ref[:, sl])
        a_ref[:, sl] = (g * jax.nn.sigmoid(g) * u).astype(BF16)


def _ffn_pair(hs, gpre, gpost, wg_ref, wu_ref, wd_ref, a_ref, before_second_down):
    n_chunks = D_FF // FF_CHUNK
    hn0, hn1 = [_rms(h, gpre).astype(BF16) for h in hs]
    _ffn_up(hn0, range(n_chunks), wg_ref, wu_ref, a_ref.at[0])
    _ffn_up(hn1, range(FFN_AHEAD), wg_ref, wu_ref, a_ref.at[1])
    out0 = hs[0] + 0.5 * _rms(_dot(a_ref[0], wd_ref[...]), gpost)
    _ffn_up(hn1, range(FFN_AHEAD, n_chunks), wg_ref, wu_ref, a_ref.at[1])
    filler = before_second_down(out0)
    out1 = hs[1] + 0.5 * _rms(_dot(a_ref[1], wd_ref[...]), gpost)
    return out0, out1, filler


def _feature_major(wt_ref, r, hn):
    return _dot_nt(wt_ref[r * ATTN_W:(r + 1) * ATTN_W, :], hn)


def _inproj_gates(hn, wt_ref):
    gv = _gelu(_feature_major(wt_ref, 3, hn))
    return _gelu(_feature_major(wt_ref, 2, hn)), gv


def _inproj_qkv(hn, rows, wk_ref, wt_ref, k_ref, qt_ref, vt_ref):
    k_ref[rows, :] = _dot(hn, wk_ref[...]).astype(BF16)
    qt_ref[0, :, rows] = _feature_major(wt_ref, 0, hn).astype(BF16)
    vt_ref[0, :, rows] = _feature_major(wt_ref, 1, hn).astype(BF16)


def _gmlp_norm(gv, lng_ref, lnb_ref, vn_ref):
    for c in range(SUB // CHUNK):
        v = gv[:, c * CHUNK:(c + 1) * CHUNK].reshape(N_GROUPS, HEAD_DIM, CHUNK)
        mu = jnp.mean(v, axis=1, keepdims=True)
        var = jnp.mean(jnp.square(v - mu), axis=1, keepdims=True)
        vn = ((v - mu) * lax.rsqrt(var + LN_EPS)).reshape(GMLP_W, CHUNK)
        vn = (vn * lng_ref[...] + lnb_ref[...]).astype(BF16).reshape(N_GROUPS, HEAD_DIM, CHUNK)
        vn_ref[:, c * HEAD_DIM:(c + 1) * HEAD_DIM, :] = vn


def _gmlp_mix(gu, rows, w_causal, bs_ref, gn_ref, o_ref, vn_ref, gated_ref):
    n_chunks = SUB // CHUNK
    ssq = jnp.zeros((n_chunks, 1, CHUNK), F32)
    for g in range(N_GROUPS):
        mixed = _dot_nt(vn_ref[g], w_causal[g]) + bs_ref[g:g + 1, :]
        feat = slice(g * HEAD_DIM, (g + 1) * HEAD_DIM)
        u = jnp.concatenate([gu[feat, c * CHUNK:(c + 1) * CHUNK] for c in range(n_chunks)], axis=0)
        gated = (u * mixed).reshape(n_chunks, HEAD_DIM, CHUNK)
        ssq = ssq + jnp.sum(gated * gated, axis=1, keepdims=True)
        gated_ref[:, feat, :] = gated
    inv = lax.rsqrt(ssq * (1.0 / GMLP_W) + RMS_EPS)
    o_ref[rows, :] = jnp.concatenate(
        [(gated_ref[c] * inv[c] * gn_ref[...]).T for c in range(n_chunks)], axis=0).astype(BF16)


def _outproj_stage(h, rows, attn_ref, gmlp_ref, gpost, an_ref, woa_ref, wog_ref):
    attn_n = _rms(attn_ref[rows, :].astype(F32), an_ref[...]).astype(BF16)
    m = _dot(attn_n, woa_ref[...]) + _dot(gmlp_ref[rows, :], wog_ref[...])
    return h + _rms(m, gpost)


def _ple_stage(h, p_proj, gpre, gpost, wg_ref):
    hn = _rms(h, gpre).astype(BF16)
    gate = jax.nn.sigmoid(_dot(hn, wg_ref[...]))
    return h + _rms(gate * p_proj, gpost)


(G_FFN1_PRE, G_FFN1_POST, G_MIX_PRE, G_MIX_POST, G_FFN2_PRE, G_FFN2_POST, G_PLE_PRE, G_PLE_POST) = range(8)
N_GAINS = 8
N_FFN_REFS = 3
N_INPROJ_REFS = 2
N_GMLP_REFS = 5
N_OUTPROJ_REFS = 3
N_PLE_REFS = 2
SUB_TILES = [slice(r * SUB, (r + 1) * SUB) for r in range(TM // SUB)]
assert len(SUB_TILES) == 2
FFN_AHEAD = 2


def _split(refs, *sizes):
    groups, at = [], 0
    for n in sizes:
        groups.append(refs[at:at + n])
        at += n
    return (*groups, refs[at:])


def _gain(gains_ref, row):
    return gains_ref[row:row + 1, :]


class _SideCast(NamedTuple):
    source: jax.Array
    in_spec: pl.BlockSpec
    out_shape: jax.ShapeDtypeStruct
    out_spec: pl.BlockSpec
    transpose: bool


def _side_cast(stacked, layer, cols=None, col_block=0):
    _, n_rows, n_cols = stacked.shape
    cols = n_cols if cols is None else cols
    steps = max(s for s in range(1, TOKENS // TM + 1)
                if n_rows % s == 0 and (n_rows // s) % BF16_SUBLANES == 0)
    rows = n_rows // steps
    return _SideCast(
        stacked,
        pl.BlockSpec((None, rows, cols), lambda i: (layer, jnp.minimum(i, steps - 1), col_block)),
        jax.ShapeDtypeStruct((n_rows, cols), BF16),
        pl.BlockSpec((rows, cols), lambda i: (jnp.minimum(i, steps - 1), 0)),
        False)


def _side_cast_inproj_transposed(w_in, layer):
    k_first, k_blocks = ATTN_W // LANES, ATTN_W // LANES
    col_blocks = 4 * ATTN_W // LANES
    row_blocks = (TOKENS // TM) // col_blocks
    rows = D_MODEL // row_blocks
    assert row_blocks >= 1 and rows % LANES == 0

    def tile(i):
        c = jnp.minimum(i // row_blocks, col_blocks - 1)
        return i % row_blocks, c, c + jnp.where(c >= k_first, k_blocks, 0)

    return _SideCast(
        w_in,
        pl.BlockSpec((None, rows, LANES), lambda i: (layer, tile(i)[0], tile(i)[2])),
        jax.ShapeDtypeStruct((4 * ATTN_W, D_MODEL), BF16),
        pl.BlockSpec((LANES, rows), lambda i: (tile(i)[1], tile(i)[0])),
        True)


def _run_side_casts(transposes, src_refs, dst_refs):
    for transpose, src, dst in zip(transposes, src_refs, dst_refs):
        x = src[...]
        dst[...] = (x.T if transpose else x).astype(BF16)


def _pre_attention_kernel(side, h_ref, gains_ref, *refs):
    (ffn, (wk_ref, wt_ref), (ws_ref, lng_ref, lnb_ref, bs_ref, gn_ref), side_in,
     (h1_ref, k_ref, qt_ref, vt_ref, gmlp_ref), side_out, (a_ref, vn_ref, gated_ref)) = _split(
        refs, N_FFN_REFS, N_INPROJ_REFS, N_GMLP_REFS, len(side), 5, len(side))
    t_idx = lax.broadcasted_iota(jnp.int32, (CHUNK, CHUNK), 0)
    s_idx = lax.broadcasted_iota(jnp.int32, (CHUNK, CHUNK), 1)
    w_causal = jnp.where(s_idx <= t_idx, ws_ref[...], 0.0).astype(BF16)

    def project_gates(r, h1):
        h1_ref[SUB_TILES[r], :] = h1
        hn = _rms(h1, _gain(gains_ref, G_MIX_PRE)).astype(BF16)
        gu, gv = _inproj_gates(hn, wt_ref)
        _gmlp_norm(gv, lng_ref, lnb_ref, vn_ref.at[r])
        return hn, gu

    def mix(r, gu):
        _gmlp_mix(gu, SUB_TILES[r], w_causal, bs_ref, gn_ref, gmlp_ref, vn_ref.at[r], gated_ref.at[r])

    def project_qkv(r, hn):
        _inproj_qkv(hn, SUB_TILES[r], wk_ref, wt_ref, k_ref, qt_ref, vt_ref)

    _, h1_second, (hn0, gu0) = _ffn_pair(
        [h_ref[rows, :] for rows in SUB_TILES], _gain(gains_ref, G_FFN1_PRE), _gain(gains_ref, G_FFN1_POST),
        *ffn, a_ref, before_second_down=functools.partial(project_gates, 0))
    _run_side_casts(side, side_in, side_out)
    project_qkv(0, hn0)
    hn1, gu1 = project_gates(1, h1_second)
    mix(0, gu0)
    project_qkv(1, hn1)
    mix(1, gu1)


def _pre_attention(h, gains, ffn, inproj, gmlp, side):
    tiles_per_seq = SEQ // TM
    row = pl.BlockSpec((TM, D_MODEL), lambda i: (i, 0))
    tok = pl.BlockSpec((TM, ATTN_W), lambda i: (i, 0))
    feat = pl.BlockSpec((1, ATTN_W, TM), lambda i: (i // tiles_per_seq, 0, i % tiles_per_seq))
    tok_shape = jax.ShapeDtypeStruct((TOKENS, ATTN_W), BF16)
    feat_shape = jax.ShapeDtypeStruct((BATCH, ATTN_W, SEQ), BF16)
    assert len(ffn) == N_FFN_REFS and len(inproj) == N_INPROJ_REFS and len(gmlp) == N_GMLP_REFS
    operands, specs = zip(gains, *ffn, *inproj, *gmlp)
    return pl.pallas_call(
        functools.partial(_pre_attention_kernel, [c.transpose for c in side]),
        grid=(TOKENS // TM,),
        in_specs=[row, *specs, *[c.in_spec for c in side]],
        out_specs=[row, tok, feat, feat, tok, *[c.out_spec for c in side]],
        out_shape=[jax.ShapeDtypeStruct((TOKENS, D_MODEL), F32), tok_shape, feat_shape, feat_shape,
                   jax.ShapeDtypeStruct((TOKENS, GMLP_W), BF16), *[c.out_shape for c in side]],
        scratch_shapes=[pltpu.VMEM((TM // SUB, SUB, D_FF), BF16),
                        pltpu.VMEM((TM // SUB, N_GROUPS, (SUB // CHUNK) * HEAD_DIM, CHUNK), BF16),
                        pltpu.VMEM((TM // SUB, SUB // CHUNK, GMLP_W, CHUNK), F32)],
        compiler_params=_params(1),
        name="ffn1_inproj_gmlp",
    )(h, *operands, *[c.source for c in side])


def _post_attention_kernel(side, h_ref, attn_ref, gmlp_ref, p_ref, gains_ref, *refs):
    outproj, ffn, ple, side_in, (o_ref,), side_out, (a_ref,) = _split(
        refs, N_OUTPROJ_REFS, N_FFN_REFS, N_PLE_REFS, len(side), 1, len(side))
    hs = [_outproj_stage(h_ref[rows, :], rows, attn_ref, gmlp_ref, _gain(gains_ref, G_MIX_POST), *outproj)
          for rows in SUB_TILES]
    wg_ref, wp_ref = ple

    def project_p(_):
        return [_dot(p_ref[rows, :].astype(BF16), wp_ref[...]) for rows in SUB_TILES]

    *hs, p_proj = _ffn_pair(hs, _gain(gains_ref, G_FFN2_PRE), _gain(gains_ref, G_FFN2_POST), *ffn, a_ref,
                            before_second_down=project_p)
    _run_side_casts(side, side_in, side_out)
    for rows, h, pp in zip(SUB_TILES, hs, p_proj):
        o_ref[rows, :] = _ple_stage(h, pp, _gain(gains_ref, G_PLE_PRE), _gain(gains_ref, G_PLE_POST), wg_ref)


def _post_attention(h, attn, gmlp, p, layer, gains, outproj, ffn, ple, side):
    row = pl.BlockSpec((TM, D_MODEL), lambda i: (i, 0))
    tok = pl.BlockSpec((TM, ATTN_W), lambda i: (i, 0))
    assert len(outproj) == N_OUTPROJ_REFS and len(ffn) == N_FFN_REFS and len(ple) == N_PLE_REFS
    operands, specs = zip(gains, *outproj, *ffn, *ple)
    return pl.pallas_call(
        functools.partial(_post_attention_kernel, [c.transpose for c in side]),
        grid=(TOKENS // TM,),
        in_specs=[row, tok, tok, pl.BlockSpec((None, TM, PLE_DIM), lambda i: (layer, i, 0)), *specs,
                  *[c.in_spec for c in side]],
        out_specs=[row, *[c.out_spec for c in side]],
        out_shape=[jax.ShapeDtypeStruct((TOKENS, D_MODEL), F32), *[c.out_shape for c in side]],
        scratch_shapes=[pltpu.VMEM((TM // SUB, SUB, D_FF), BF16)],
        compiler_params=_params(1),
        name="outproj_ffn2_ple",
    )(h, attn, gmlp, p, *operands, *[c.source for c in side])


SPLIT = 4
POS_ROWS = 4 * SPLIT
CHOICE_ROWS = N_BLK
AUX_CHOICE = POS_ROWS
SCORE_LOOKAHEAD = 4
PV_DELAY = 1
PAIRS_PER_STEP = 4
SUM_ROWS = 16
LOG2E = float(np.log2(np.e))


def _split_bf16(x):
    terms, rest = [], x
    for _ in range(SPLIT):
        t = rest.astype(BF16).astype(np.float64)
        terms.append(t)
        rest = rest - t
    assert np.max(np.abs(rest)) <= 1e-7 * max(1.0, np.max(np.abs(x)))
    return terms


def _attention_constants():
    start = 2.0 ** (-8.0 / N_HEADS)
    slopes = np.array([start ** (i + 1) for i in range(N_HEADS)], dtype=np.float32).astype(np.float64)
    pos = np.arange(SEQ)
    local = (pos % BLK).astype(np.float64)
    blk = pos // BLK
    kaux = np.zeros((N_HEADS, SEQ, PAIR_W), np.float64)
    qaux = np.zeros((N_HEADS, POS_ROWS, SEQ), np.float64)
    for h in range(N_HEADS):
        base = (1 - h % 2) * HEAD_DIM
        c = LOG2E * slopes[h]
        fine, coarse = _split_bf16(c * local), _split_bf16(c * BLK * blk)
        for n in range(SPLIT):
            kaux[h, :, base + n] = 1.0
            qaux[h, n] = -fine[n]
            kaux[h, :, base + SPLIT + n] = fine[n]
            qaux[h, SPLIT + n] = 1.0
            kaux[h, :, base + 2 * SPLIT + n] = 1.0
            qaux[h, 2 * SPLIT + n] = -coarse[n]
            kaux[h, :, base + 3 * SPLIT + n] = coarse[n]
            qaux[h, 3 * SPLIT + n] = 1.0
        kaux[h, pos, base + AUX_CHOICE + blk] = 1.0
    assert np.array_equal(kaux.astype(BF16).astype(np.float64), kaux)
    assert np.array_equal(qaux.astype(BF16).astype(np.float64), qaux)
    return jnp.asarray(kaux.astype(BF16)), jnp.asarray(qaux.astype(np.float32))


def _attn_kernel(kaux_ref, qaux_ref, k_ref, qt_ref, vt_ref, o_ref, kaug_ref, choice_ref, vaug_ref):
    lane = lax.broadcasted_iota(jnp.int32, (1, PAIR_W), 1)
    blk_of_row = lax.broadcasted_iota(jnp.int32, (N_BLK, SEQ), 0)
    blk_of_query = lax.broadcasted_iota(jnp.int32, (N_BLK, SEQ), 1) // BLK
    in_past = blk_of_row < blk_of_query
    first_pair = pl.program_id(1) * PAIRS_PER_STEP

    def pair_slice(pp):
        return slice(pp * PAIR_W, (pp + 1) * PAIR_W)

    def prepare_pair(pp):
        k_all = k_ref[:, pair_slice(pp)]
        q_all = qt_ref[0, pair_slice(pp), :]
        k_mean = jnp.mean(k_all.astype(F32).reshape(N_BLK, BLK, PAIR_W), axis=1)
        for a in range(2):
            head = 2 * (first_pair + pp) + a
            in_head = (lane >= a * HEAD_DIM) & (lane < (a + 1) * HEAD_DIM)
            kaug_ref[pp, a] = jnp.where(in_head, k_all, kaux_ref[head])
            km = jnp.where(in_head, k_mean, 0.0)
            km_hi = km.astype(BF16)
            km_lo = (km - km_hi.astype(F32)).astype(BF16)
            gate = _dot(km_hi, q_all) + _dot(km_lo, q_all)
            gate = jnp.where(in_past, gate, NEG)
            rank = jnp.zeros((N_BLK, SEQ), jnp.int32)
            for other in range(N_BLK):
                row = gate[other:other + 1, :]
                beats = (row > gate) | ((row == gate) & (other < blk_of_row))
                rank = rank + beats.astype(jnp.int32)
            chosen = (rank < TOP_K) & (gate > 0.5 * NEG)
            choice_ref[pp, a] = jnp.where(in_past & jnp.logical_not(chosen), NEG, 0.0)
            vaug_ref[pp, a, :HEAD_DIM, :] = vt_ref[0, pl.ds(pp * PAIR_W + a * HEAD_DIM, HEAD_DIM), :]
            vaug_ref[pp, a, HEAD_DIM:, :] = jnp.ones((SUM_ROWS, SEQ), BF16)

    key_idx = lax.broadcasted_iota(jnp.int32, (BLK, BLK), 0)
    query_idx = lax.broadcasted_iota(jnp.int32, (BLK, BLK), 1)
    causal = key_idx <= query_idx
    aux_fill = jnp.zeros((HEAD_DIM - POS_ROWS - CHOICE_ROWS, BLK), F32)

    def blk_slice(j):
        return slice(j * BLK, (j + 1) * BLK)

    tiles = [(pp, i, a, j) for pp in range(PAIRS_PER_STEP) for i in range(N_BLK)
             for j in [i] + list(range(i)) for a in range(2)]
    q_augs, scores, probs, running_max, acc_state, outs = {}, {}, {}, {}, {}, {}

    def issue_scores(n):
        pp, i, a, j = tiles[n]
        if (pp, i, a) not in q_augs:
            head = 2 * (first_pair + pp) + a
            feat = pl.ds(pp * PAIR_W + a * HEAD_DIM, HEAD_DIM)
            q_head = qt_ref[0, feat, blk_slice(i)].astype(F32) * (LOG2E * HEAD_DIM ** -0.5)
            aux = jnp.concatenate(
                [qaux_ref[head, :, blk_slice(i)], choice_ref[pp, a, :, blk_slice(i)], aux_fill], axis=0)
            q_augs[pp, i, a] = jnp.concatenate(
                [q_head, aux] if a == 0 else [aux, q_head], axis=0).astype(BF16)
        scores[n] = _dot(kaug_ref[pp, a, blk_slice(j), :], q_augs[pp, i, a])

    def softmax_tile(n):
        pp, i, a, j = tiles[n]
        s = scores.pop(n)
        if j == i:
            s = jnp.where(causal, s, NEG)
            m = jnp.max(s, axis=0, keepdims=True)
            alpha = None
        else:
            m_old = running_max[pp, i, a]
            m = jnp.maximum(m_old, jnp.max(s, axis=0, keepdims=True))
            alpha = jnp.exp2(m_old - m)
        running_max[pp, i, a] = m
        probs[n] = (jnp.exp2(s - m).astype(BF16), alpha)

    def weighted_values(n):
        pp, i, a, j = tiles[n]
        p, alpha = probs.pop(n)
        pv = _dot(vaug_ref[pp, a, :, blk_slice(j)], p)
        acc = pv if alpha is None else alpha * acc_state[pp, i, a] + pv
        acc_state[pp, i, a] = acc
        if j == (i - 1 if i > 0 else 0):
            outs.setdefault((pp, i), []).append(acc[:HEAD_DIM] / acc[HEAD_DIM:HEAD_DIM + 1])
            del acc_state[pp, i, a]
            if a == 1:
                o_ref[blk_slice(i), pair_slice(pp)] = jnp.concatenate(outs.pop((pp, i)), axis=0).T.astype(BF16)

    prepare_pair(0)
    for n in range(min(SCORE_LOOKAHEAD, len(tiles))):
        issue_scores(n)
    for n in range(len(tiles) + PV_DELAY):
        if n < len(tiles):
            pp, i, a, j = tiles[n]
            if (i, a, j) == (N_BLK - 1, 0, N_BLK - 1) and pp + 1 < PAIRS_PER_STEP:
                prepare_pair(pp + 1)
        if n + SCORE_LOOKAHEAD < len(tiles):
            issue_scores(n + SCORE_LOOKAHEAD)
        if n < len(tiles):
            softmax_tile(n)
        if n >= PV_DELAY:
            weighted_values(n - PV_DELAY)


def _attention(kaux, qaux, k, qt, vt):
    step_w = PAIRS_PER_STEP * PAIR_W
    tok = pl.BlockSpec((SEQ, step_w), lambda b, p: (b, p))
    feat = pl.BlockSpec((1, step_w, SEQ), lambda b, p: (b, p, 0))
    return pl.pallas_call(
        _attn_kernel,
        grid=(BATCH, ATTN_W // step_w),
        in_specs=[_whole(kaux)[1], _whole(qaux)[1], tok, feat, feat],
        out_specs=tok,
        out_shape=jax.ShapeDtypeStruct((TOKENS, ATTN_W), BF16),
        scratch_shapes=[pltpu.VMEM((PAIRS_PER_STEP, 2, SEQ, PAIR_W), BF16),
                        pltpu.VMEM((PAIRS_PER_STEP, 2, N_BLK, SEQ), F32),
                        pltpu.VMEM((PAIRS_PER_STEP, 2, HEAD_DIM + SUM_ROWS, SEQ), BF16)],
        compiler_params=_params(2),
        name="moba_attention",
    )(kaux, qaux, k, qt, vt)


CAST_ROWS = 256


def _cast_kernel(*refs):
    n = len(refs) // 2
    for src, dst in zip(refs[:n], refs[n:]):
        dst[...] = src[...].astype(BF16)


def _cast_bf16(layer, *stacked):
    _, rows, cols = stacked[0].shape
    assert all(w.shape == stacked[0].shape for w in stacked) and rows % CAST_ROWS == 0
    return pl.pallas_call(
        _cast_kernel,
        grid=(rows // CAST_ROWS,),
        in_specs=[pl.BlockSpec((None, CAST_ROWS, cols), lambda r: (layer, r, 0))] * len(stacked),
        out_specs=[pl.BlockSpec((CAST_ROWS, cols), lambda r: (r, 0))] * len(stacked),
        out_shape=[jax.ShapeDtypeStruct((rows, cols), BF16) for _ in stacked],
        compiler_params=_params(1),
        name="cast_bf16",
    )(*stacked)


def _transpose_cast_kernel(w_ref, o_ref):
    o_ref[...] = w_ref[...].T.astype(BF16)


def _inproj_weights_transposed(w_in, layer):
    return pl.pallas_call(
        _transpose_cast_kernel,
        grid=(4,),
        in_specs=[pl.BlockSpec((None, D_MODEL, ATTN_W), lambda r: (layer, 0, r + jnp.minimum(r, 1)))],
        out_specs=pl.BlockSpec((ATTN_W, D_MODEL), lambda r: (r, 0)),
        out_shape=jax.ShapeDtypeStruct((4 * ATTN_W, D_MODEL), BF16),
        compiler_params=_params(1),
        name="inproj_weights_t",
    )(w_in)


def kernel(x, p, ffn1_pre_norm, ffn1_w_gate, ffn1_w_up, ffn1_w_down, ffn1_post_norm, mix_pre_norm, w_in, gmlp_ln_g, gmlp_ln_b, gmlp_w_s, gmlp_b_s, attn_out_norm, gmlp_out_norm, w_out, mix_post_norm, ffn2_pre_norm, ffn2_w_gate, ffn2_w_up, ffn2_w_down, ffn2_post_norm, ple_pre_norm, ple_w_gate, ple_w_proj, ple_post_norm):
    assert x.shape == (BATCH, SEQ, D_MODEL) and p.shape == (DEPTH, BATCH, SEQ, PLE_DIM)
    assert ATTN_W == GMLP_W and w_in.shape == (DEPTH, D_MODEL, 5 * ATTN_W)
    kaux, qaux = _attention_constants()
    h = x.reshape(TOKENS, D_MODEL)
    p = p.reshape(DEPTH, TOKENS, PLE_DIM)

    def col(v):
        return jnp.broadcast_to(v[:, :, None], (*v.shape, CHUNK))

    gains = jnp.stack([ffn1_pre_norm, ffn1_post_norm, mix_pre_norm, mix_post_norm,
                       ffn2_pre_norm, ffn2_post_norm, ple_pre_norm, ple_post_norm], axis=1)
    assert gains.shape == (DEPTH, N_GAINS, D_MODEL)
    ln_g, ln_b, gn = col(gmlp_ln_g), col(gmlp_ln_b), col(gmlp_out_norm)
    attn_gain = attn_out_norm.reshape(DEPTH, 1, ATTN_W)

    f1g, f1u = _cast_bf16(0, ffn1_w_gate, ffn1_w_up)
    (f1d,) = _cast_bf16(0, ffn1_w_down)
    w_t = _inproj_weights_transposed(w_in, 0)
    w_k = w_in[0, :, ATTN_W:2 * ATTN_W].astype(BF16)

    for i in range(DEPTH):
        def layer(w):
            return _layer_operand(w, i)

        post_weights = [_side_cast(w, i) for w in (ffn2_w_gate, ffn2_w_up, ffn2_w_down, w_out, ple_w_gate, ple_w_proj)]
        gmlp = (layer(gmlp_w_s), layer(ln_g), layer(ln_b), layer(gmlp_b_s), layer(gn))
        h, k, qt, vt, gmlp_n, f2g, f2u, f2d, wo, pwg, pwp = _pre_attention(
            h, layer(gains), (_whole(f1g), _whole(f1u), _whole(f1d)), (_whole(w_k), _whole(w_t)), gmlp,
            post_weights)
        attn = _attention(kaux, qaux, k, qt, vt)
        next_weights = [] if i + 1 == DEPTH else [
            _side_cast(ffn1_w_gate, i + 1), _side_cast(ffn1_w_up, i + 1), _side_cast(ffn1_w_down, i + 1),
            _side_cast(w_in, i + 1, cols=ATTN_W, col_block=1), _side_cast_inproj_transposed(w_in, i + 1)]
        outproj = (layer(attn_gain), _whole(wo, (ATTN_W, D_MODEL), (0, 0)), _whole(wo, (GMLP_W, D_MODEL), (1, 0)))
        h, *next_bf16 = _post_attention(h, attn, gmlp_n, p, i, layer(gains), outproj,
                                        (_whole(f2g), _whole(f2u), _whole(f2d)), (_whole(pwg), _whole(pwp)),
                                        next_weights)
        if next_bf16:
            f1g, f1u, f1d, w_k, w_t = next_bf16
    return h.reshape(BATCH, SEQ, D_MODEL)
```

```python
import functools
from typing import NamedTuple

import numpy as np
import jax
import jax.numpy as jnp
from jax import lax
from jax.experimental import pallas as pl
from jax.experimental.pallas import tpu as pltpu

D_MODEL = 1024
BATCH = 16
SEQ = 2048
DEPTH = 2
HEAD_DIM = 64
N_HEADS = 8
N_GROUPS = 8
ATTN_W = N_HEADS * HEAD_DIM
GMLP_W = N_GROUPS * HEAD_DIM
BLK = 256
N_BLK = SEQ // BLK
TOP_K = 3
CHUNK = 128
D_FF = 2816
PLE_DIM = 256
RMS_EPS = 1e-6
LN_EPS = 1e-5
NEG = -1e30
TOKENS = BATCH * SEQ

LANES = 128
BF16_SUBLANES = 16
PAIR_W = 2 * HEAD_DIM
TM = 1024
SUB = 256
FF_CHUNK = 256
VMEM_LIMIT = 60 * 1024 * 1024

F32 = jnp.float32
BF16 = jnp.bfloat16


def _rms(x, g):
    return x * lax.rsqrt(jnp.mean(x * x, axis=-1, keepdims=True) + RMS_EPS) * g


def _gelu(x):
    return 0.5 * x * (1.0 + lax.erf(x * np.float32(np.sqrt(0.5))))


def _dot(a, b):
    return jnp.dot(a, b, preferred_element_type=F32)


def _dot_nt(a, b):
    return lax.dot_general(a, b, (((1,), (1,)), ((), ())), preferred_element_type=F32)


def _layer_operand(stacked, layer):
    block = stacked.shape[1:]
    spec = pl.BlockSpec((None, *block), lambda *_: (layer, *(0,) * len(block)), pipeline_mode=pl.Buffered(1))
    return stacked, spec


def _whole(array, block=None, index=None):
    block = array.shape if block is None else block
    index = (0,) * len(block) if index is None else index
    return array, pl.BlockSpec(block, lambda *_: index, pipeline_mode=pl.Buffered(1))


def _params(n_axes):
    return pltpu.CompilerParams(
        dimension_semantics=("arbitrary",) * n_axes, vmem_limit_bytes=VMEM_LIMIT)


def _ffn_up(hn, chunks, wg_ref, wu_ref, a_ref):
    for c in chunks:
        sl = slice(c * FF_CHUNK, (c + 1) * FF_CHUNK)
        g = _dot(hn, wg_ref[:, sl])
        u = _dot(hn, wu_ref[:, sl])
        a_ref[:, sl] = (g * jax.nn.sigmoid(g) * u).astype(BF16)


def _ffn_pair(hs, gpre, gpost, wg_ref, wu_ref, wd_ref, a_ref, before_second_down):
    n_chunks = D_FF // FF_CHUNK
    hn0, hn1 = [_rms(h, gpre).astype(BF16) for h in hs]
    _ffn_up(hn0, range(n_chunks), wg_ref, wu_ref, a_ref.at[0])
    _ffn_up(hn1, range(FFN_AHEAD), wg_ref, wu_ref, a_ref.at[1])
    out0 = hs[0] + 0.5 * _rms(_dot(a_ref[0], wd_ref[...]), gpost)
    _ffn_up(hn1, range(FFN_AHEAD, n_chunks), wg_ref, wu_ref, a_ref.at[1])
    filler = before_second_down(out0)
    out1 = hs[1] + 0.5 * _rms(_dot(a_ref[1], wd_ref[...]), gpost)
    return out0, out1, filler


def _feature_major(wt_ref, r, hn):
    return _dot_nt(wt_ref[r * ATTN_W:(r + 1) * ATTN_W, :], hn)


def _inproj_gates(hn, wt_ref):
    gv = _gelu(_feature_major(wt_ref, 3, hn))
    return _gelu(_feature_major(wt_ref, 2, hn)), gv


def _inproj_qkv(hn, rows, wk_ref, wt_ref, k_ref, qt_ref, vt_ref):
    k_ref[rows, :] = _dot(hn, wk_ref[...]).astype(BF16)
    qt_ref[0, :, rows] = _feature_major(wt_ref, 0, hn).astype(BF16)
    vt_ref[0, :, rows] = _feature_major(wt_ref, 1, hn).astype(BF16)


def _gmlp_norm(gv, lng_ref, lnb_ref, vn_ref):
    for c in range(SUB // CHUNK):
        v = gv[:, c * CHUNK:(c + 1) * CHUNK].reshape(N_GROUPS, HEAD_DIM, CHUNK)
        mu = jnp.mean(v, axis=1, keepdims=True)
        var = jnp.mean(jnp.square(v - mu), axis=1, keepdims=True)
        vn = ((v - mu) * lax.rsqrt(var + LN_EPS)).reshape(GMLP_W, CHUNK)
        vn = (vn * lng_ref[...] + lnb_ref[...]).astype(BF16).reshape(N_GROUPS, HEAD_DIM, CHUNK)
        vn_ref[:, c * HEAD_DIM:(c + 1) * HEAD_DIM, :] = vn


def _gmlp_mix(gu, rows, w_causal, bs_ref, gn_ref, o_ref, vn_ref, gated_ref):
    n_chunks = SUB // CHUNK
    ssq = jnp.zeros((n_chunks, 1, CHUNK), F32)
    for g in range(N_GROUPS):
        mixed = _dot_nt(vn_ref[g], w_causal[g]) + bs_ref[g:g + 1, :]
        feat = slice(g * HEAD_DIM, (g + 1) * HEAD_DIM)
        u = jnp.concatenate([gu[feat, c * CHUNK:(c + 1) * CHUNK] for c in range(n_chunks)], axis=0)
        gated = (u * mixed).reshape(n_chunks, HEAD_DIM, CHUNK)
        ssq = ssq + jnp.sum(gated * gated, axis=1, keepdims=True)
        gated_ref[:, feat, :] = gated
    inv = lax.rsqrt(ssq * (1.0 / GMLP_W) + RMS_EPS)
    o_ref[rows, :] = jnp.concatenate(
        [(gated_ref[c] * inv[c] * gn_ref[...]).T for c in range(n_chunks)], axis=0).astype(BF16)


def _outproj_stage(h, rows, attn_ref, gmlp_ref, gpost, an_ref, woa_ref, wog_ref):
    attn_n = _rms(attn_ref[rows, :].astype(F32), an_ref[...]).astype(BF16)
    m = _dot(attn_n, woa_ref[...]) + _dot(gmlp_ref[rows, :], wog_ref[...])
    return h + _rms(m, gpost)


def _ple_stage(h, p_proj, gpre, gpost, wg_ref):
    hn = _rms(h, gpre).astype(BF16)
    gate = jax.nn.sigmoid(_dot(hn, wg_ref[...]))
    return h + _rms(gate * p_proj, gpost)


(G_FFN1_PRE, G_FFN1_POST, G_MIX_PRE, G_MIX_POST, G_FFN2_PRE, G_FFN2_POST, G_PLE_PRE, G_PLE_POST) = range(8)
N_GAINS = 8
N_FFN_REFS = 3
N_INPROJ_REFS = 2
N_GMLP_REFS = 5
N_OUTPROJ_REFS = 3
N_PLE_REFS = 2
SUB_TILES = [slice(r * SUB, (r + 1) * SUB) for r in range(TM // SUB)]
SUB_TILE_PAIRS = [SUB_TILES[r:r + 2] for r in range(0, len(SUB_TILES), 2)]
assert all(len(pair) == 2 for pair in SUB_TILE_PAIRS)
FFN_AHEAD = 2


def _split(refs, *sizes):
    groups, at = [], 0
    for n in sizes:
        groups.append(refs[at:at + n])
        at += n
    return (*groups, refs[at:])


def _gain(gains_ref, row):
    return gains_ref[row:row + 1, :]


class _SideCast(NamedTuple):
    source: jax.Array
    in_spec: pl.BlockSpec
    out_shape: jax.ShapeDtypeStruct
    out_spec: pl.BlockSpec
    transpose: bool


def _side_cast(stacked, layer, cols=None, col_block=0):
    _, n_rows, n_cols = stacked.shape
    cols = n_cols if cols is None else cols
    steps = max(s for s in range(1, TOKENS // TM + 1)
                if n_rows % s == 0 and (n_rows // s) % BF16_SUBLANES == 0)
    rows = n_rows // steps
    return _SideCast(
        stacked,
        pl.BlockSpec((None, rows, cols), lambda i: (layer, jnp.minimum(i, steps - 1), col_block)),
        jax.ShapeDtypeStruct((n_rows, cols), BF16),
        pl.BlockSpec((rows, cols), lambda i: (jnp.minimum(i, steps - 1), 0)),
        False)


def _side_cast_inproj_transposed(w_in, layer):
    k_first, k_blocks = ATTN_W // LANES, ATTN_W // LANES
    col_blocks = 4 * ATTN_W // LANES
    row_blocks = (TOKENS // TM) // col_blocks
    rows = D_MODEL // row_blocks
    assert row_blocks >= 1 and rows % LANES == 0

    def tile(i):
        c = jnp.minimum(i // row_blocks, col_blocks - 1)
        return i % row_blocks, c, c + jnp.where(c >= k_first, k_blocks, 0)

    return _SideCast(
        w_in,
        pl.BlockSpec((None, rows, LANES), lambda i: (layer, tile(i)[0], tile(i)[2])),
        jax.ShapeDtypeStruct((4 * ATTN_W, D_MODEL), BF16),
        pl.BlockSpec((LANES, rows), lambda i: (tile(i)[1], tile(i)[0])),
        True)


def _run_side_casts(transposes, src_refs, dst_refs):
    for transpose, src, dst in zip(transposes, src_refs, dst_refs):
        x = src[...]
        dst[...] = (x.T if transpose else x).astype(BF16)


def _pre_attention_kernel(side, h_ref, gains_ref, *refs):
    (ffn, (wk_ref, wt_ref), (ws_ref, lng_ref, lnb_ref, bs_ref, gn_ref), side_in,
     (h1_ref, k_ref, qt_ref, vt_ref, gmlp_ref), side_out, (a_ref, vn_ref, gated_ref)) = _split(
        refs, N_FFN_REFS, N_INPROJ_REFS, N_GMLP_REFS, len(side), 5, len(side))
    t_idx = lax.broadcasted_iota(jnp.int32, (CHUNK, CHUNK), 0)
    s_idx = lax.broadcasted_iota(jnp.int32, (CHUNK, CHUNK), 1)
    w_causal = jnp.where(s_idx <= t_idx, ws_ref[...], 0.0).astype(BF16)

    def process(pair, with_side_casts):
        def project_gates(r, h1):
            h1_ref[pair[r], :] = h1
            hn = _rms(h1, _gain(gains_ref, G_MIX_PRE)).astype(BF16)
            gu, gv = _inproj_gates(hn, wt_ref)
            _gmlp_norm(gv, lng_ref, lnb_ref, vn_ref.at[r])
            return hn, gu

        def mix(r, gu):
            _gmlp_mix(gu, pair[r], w_causal, bs_ref, gn_ref, gmlp_ref, vn_ref.at[r], gated_ref.at[r])

        def project_qkv(r, hn):
            _inproj_qkv(hn, pair[r], wk_ref, wt_ref, k_ref, qt_ref, vt_ref)

        _, h1_second, (hn0, gu0) = _ffn_pair(
            [h_ref[rows, :] for rows in pair], _gain(gains_ref, G_FFN1_PRE), _gain(gains_ref, G_FFN1_POST),
            *ffn, a_ref, before_second_down=functools.partial(project_gates, 0))
        if with_side_casts:
            _run_side_casts(side, side_in, side_out)
        project_qkv(0, hn0)
        hn1, gu1 = project_gates(1, h1_second)
        mix(0, gu0)
        project_qkv(1, hn1)
        mix(1, gu1)

    for n, pair in enumerate(SUB_TILE_PAIRS):
        process(pair, with_side_casts=(n == 0))


def _pre_attention(h, gains, ffn, inproj, gmlp, side):
    tiles_per_seq = SEQ // TM
    row = pl.BlockSpec((TM, D_MODEL), lambda i: (i, 0))
    tok = pl.BlockSpec((TM, ATTN_W), lambda i: (i, 0))
    feat = pl.BlockSpec((1, ATTN_W, TM), lambda i: (i // tiles_per_seq, 0, i % tiles_per_seq))
    tok_shape = jax.ShapeDtypeStruct((TOKENS, ATTN_W), BF16)
    feat_shape = jax.ShapeDtypeStruct((BATCH, ATTN_W, SEQ), BF16)
    assert len(ffn) == N_FFN_REFS and len(inproj) == N_INPROJ_REFS and len(gmlp) == N_GMLP_REFS
    operands, specs = zip(gains, *ffn, *inproj, *gmlp)
    return pl.pallas_call(
        functools.partial(_pre_attention_kernel, [c.transpose for c in side]),
        grid=(TOKENS // TM,),
        in_specs=[row, *specs, *[c.in_spec for c in side]],
        out_specs=[row, tok, feat, feat, tok, *[c.out_spec for c in side]],
        out_shape=[jax.ShapeDtypeStruct((TOKENS, D_MODEL), F32), tok_shape, feat_shape, feat_shape,
                   jax.ShapeDtypeStruct((TOKENS, GMLP_W), BF16), *[c.out_shape for c in side]],
        scratch_shapes=[pltpu.VMEM((2, SUB, D_FF), BF16),
                        pltpu.VMEM((2, N_GROUPS, (SUB // CHUNK) * HEAD_DIM, CHUNK), BF16),
                        pltpu.VMEM((2, SUB // CHUNK, GMLP_W, CHUNK), F32)],
        compiler_params=_params(1),
        name="ffn1_inproj_gmlp",
    )(h, *operands, *[c.source for c in side])


def _post_attention_kernel(side, h_ref, attn_ref, gmlp_ref, p_ref, gains_ref, *refs):
    outproj, ffn, ple, side_in, (o_ref,), side_out, (a_ref,) = _split(
        refs, N_OUTPROJ_REFS, N_FFN_REFS, N_PLE_REFS, len(side), 1, len(side))
    wg_ref, wp_ref = ple
    for n, pair in enumerate(SUB_TILE_PAIRS):
        hs = [_outproj_stage(h_ref[rows, :], rows, attn_ref, gmlp_ref, _gain(gains_ref, G_MIX_POST), *outproj)
              for rows in pair]

        def project_p(_, pair=pair):
            return [_dot(p_ref[rows, :].astype(BF16), wp_ref[...]) for rows in pair]

        *hs, p_proj = _ffn_pair(hs, _gain(gains_ref, G_FFN2_PRE), _gain(gains_ref, G_FFN2_POST), *ffn, a_ref,
                                before_second_down=project_p)
        if n == 0:
            _run_side_casts(side, side_in, side_out)
        for rows, h, pp in zip(pair, hs, p_proj):
            o_ref[rows, :] = _ple_stage(h, pp, _gain(gains_ref, G_PLE_PRE), _gain(gains_ref, G_PLE_POST), wg_ref)


def _post_attention(h, attn, gmlp, p, layer, gains, outproj, ffn, ple, side):
    row = pl.BlockSpec((TM, D_MODEL), lambda i: (i, 0))
    tok = pl.BlockSpec((TM, ATTN_W), lambda i: (i, 0))
    assert len(outproj) == N_OUTPROJ_REFS and len(ffn) == N_FFN_REFS and len(ple) == N_PLE_REFS
    operands, specs = zip(gains, *outproj, *ffn, *ple)
    return pl.pallas_call(
        functools.partial(_post_attention_kernel, [c.transpose for c in side]),
        grid=(TOKENS // TM,),
        in_specs=[row, tok, tok, pl.BlockSpec((None, TM, PLE_DIM), lambda i: (layer, i, 0)), *specs,
                  *[c.in_spec for c in side]],
        out_specs=[row, *[c.out_spec for c in side]],
        out_shape=[jax.ShapeDtypeStruct((TOKENS, D_MODEL), F32), *[c.out_shape for c in side]],
        scratch_shapes=[pltpu.VMEM((2, SUB, D_FF), BF16)],
        compiler_params=_params(1),
        name="outproj_ffn2_ple",
    )(h, attn, gmlp, p, *operands, *[c.source for c in side])


SPLIT = 4
POS_ROWS = 4 * SPLIT
CHOICE_ROWS = N_BLK
AUX_CHOICE = POS_ROWS
SCORE_LOOKAHEAD = 4
PV_DELAY = 1
PAIRS_PER_STEP = 4
SUM_ROWS = 16
LOG2E = float(np.log2(np.e))


def _split_bf16(x):
    terms, rest = [], x
    for _ in range(SPLIT):
        t = rest.astype(BF16).astype(np.float64)
        terms.append(t)
        rest = rest - t
    assert np.max(np.abs(rest)) <= 1e-7 * max(1.0, np.max(np.abs(x)))
    return terms


def _attention_constants():
    start = 2.0 ** (-8.0 / N_HEADS)
    slopes = np.array([start ** (i + 1) for i in range(N_HEADS)], dtype=np.float32).astype(np.float64)
    pos = np.arange(SEQ)
    local = (pos % BLK).astype(np.float64)
    blk = pos // BLK
    kaux = np.zeros((N_HEADS, SEQ, PAIR_W), np.float64)
    qaux = np.zeros((N_HEADS, POS_ROWS, SEQ), np.float64)
    for h in range(N_HEADS):
        base = (1 - h % 2) * HEAD_DIM
        c = LOG2E * slopes[h]
        fine, coarse = _split_bf16(c * local), _split_bf16(c * BLK * blk)
        for n in range(SPLIT):
            kaux[h, :, base + n] = 1.0
            qaux[h, n] = -fine[n]
            kaux[h, :, base + SPLIT + n] = fine[n]
            qaux[h, SPLIT + n] = 1.0
            kaux[h, :, base + 2 * SPLIT + n] = 1.0
            qaux[h, 2 * SPLIT + n] = -coarse[n]
            kaux[h, :, base + 3 * SPLIT + n] = coarse[n]
            qaux[h, 3 * SPLIT + n] = 1.0
        kaux[h, pos, base + AUX_CHOICE + blk] = 1.0
    assert np.array_equal(kaux.astype(BF16).astype(np.float64), kaux)
    assert np.array_equal(qaux.astype(BF16).astype(np.float64), qaux)
    return jnp.asarray(kaux.astype(BF16)), jnp.asarray(qaux.astype(np.float32))


def _attn_kernel(kaux_ref, qaux_ref, k_ref, qt_ref, vt_ref, o_ref, kaug_ref, choice_ref, vaug_ref):
    lane = lax.broadcasted_iota(jnp.int32, (1, PAIR_W), 1)
    blk_of_row = lax.broadcasted_iota(jnp.int32, (N_BLK, SEQ), 0)
    blk_of_query = lax.broadcasted_iota(jnp.int32, (N_BLK, SEQ), 1) // BLK
    in_past = blk_of_row < blk_of_query
    first_pair = pl.program_id(1) * PAIRS_PER_STEP

    def pair_slice(pp):
        return slice(pp * PAIR_W, (pp + 1) * PAIR_W)

    def prepare_pair(pp):
        k_all = k_ref[:, pair_slice(pp)]
        q_all = qt_ref[0, pair_slice(pp), :]
        k_mean = jnp.mean(k_all.astype(F32).reshape(N_BLK, BLK, PAIR_W), axis=1)
        for a in range(2):
            head = 2 * (first_pair + pp) + a
            in_head = (lane >= a * HEAD_DIM) & (lane < (a + 1) * HEAD_DIM)
            kaug_ref[pp, a] = jnp.where(in_head, k_all, kaux_ref[head])
            km = jnp.where(in_head, k_mean, 0.0)
            km_hi = km.astype(BF16)
            km_lo = (km - km_hi.astype(F32)).astype(BF16)
            gate = _dot(km_hi, q_all) + _dot(km_lo, q_all)
            gate = jnp.where(in_past, gate, NEG)
            rank = jnp.zeros((N_BLK, SEQ), jnp.int32)
            for other in range(N_BLK):
                row = gate[other:other + 1, :]
                beats = (row > gate) | ((row == gate) & (other < blk_of_row))
                rank = rank + beats.astype(jnp.int32)
            chosen = (rank < TOP_K) & (gate > 0.5 * NEG)
            choice_ref[pp, a] = jnp.where(in_past & jnp.logical_not(chosen), NEG, 0.0)
            vaug_ref[pp, a, :HEAD_DIM, :] = vt_ref[0, pl.ds(pp * PAIR_W + a * HEAD_DIM, HEAD_DIM), :]
            vaug_ref[pp, a, HEAD_DIM:, :] = jnp.ones((SUM_ROWS, SEQ), BF16)

    key_idx = lax.broadcasted_iota(jnp.int32, (BLK, BLK), 0)
    query_idx = lax.broadcasted_iota(jnp.int32, (BLK, BLK), 1)
    causal = key_idx <= query_idx
    aux_fill = jnp.zeros((HEAD_DIM - POS_ROWS - CHOICE_ROWS, BLK), F32)

    def blk_slice(j):
        return slice(j * BLK, (j + 1) * BLK)

    tiles = [(pp, i, a, j) for pp in range(PAIRS_PER_STEP) for i in range(N_BLK)
             for j in [i] + list(range(i)) for a in range(2)]
    q_augs, scores, probs, running_max, acc_state, outs = {}, {}, {}, {}, {}, {}

    def issue_scores(n):
        pp, i, a, j = tiles[n]
        if (pp, i, a) not in q_augs:
            head = 2 * (first_pair + pp) + a
            feat = pl.ds(pp * PAIR_W + a * HEAD_DIM, HEAD_DIM)
            q_head = qt_ref[0, feat, blk_slice(i)].astype(F32) * (LOG2E * HEAD_DIM ** -0.5)
            aux = jnp.concatenate(
                [qaux_ref[head, :, blk_slice(i)], choice_ref[pp, a, :, blk_slice(i)], aux_fill], axis=0)
            q_augs[pp, i, a] = jnp.concatenate(
                [q_head, aux] if a == 0 else [aux, q_head], axis=0).astype(BF16)
        scores[n] = _dot(kaug_ref[pp, a, blk_slice(j), :], q_augs[pp, i, a])

    def softmax_tile(n):
        pp, i, a, j = tiles[n]
        s = scores.pop(n)
        if j == i:
            s = jnp.where(causal, s, NEG)
            m = jnp.max(s, axis=0, keepdims=True)
            alpha = None
        else:
            m_old = running_max[pp, i, a]
            m = jnp.maximum(m_old, jnp.max(s, axis=0, keepdims=True))
            alpha = jnp.exp2(m_old - m)
        running_max[pp, i, a] = m
        probs[n] = (jnp.exp2(s - m).astype(BF16), alpha)

    def weighted_values(n):
        pp, i, a, j = tiles[n]
        p, alpha = probs.pop(n)
        pv = _dot(vaug_ref[pp, a, :, blk_slice(j)], p)
        acc = pv if alpha is None else alpha * acc_state[pp, i, a] + pv
        acc_state[pp, i, a] = acc
        if j == (i - 1 if i > 0 else 0):
            outs.setdefault((pp, i), []).append(acc[:HEAD_DIM] / acc[HEAD_DIM:HEAD_DIM + 1])
            del acc_state[pp, i, a]
            if a == 1:
                o_ref[blk_slice(i), pair_slice(pp)] = jnp.concatenate(outs.pop((pp, i)), axis=0).T.astype(BF16)

    prepare_pair(0)
    for n in range(min(SCORE_LOOKAHEAD, len(tiles))):
        issue_scores(n)
    for n in range(len(tiles) + PV_DELAY):
        if n < len(tiles):
            pp, i, a, j = tiles[n]
            if (i, a, j) == (N_BLK - 1, 0, N_BLK - 1) and pp + 1 < PAIRS_PER_STEP:
                prepare_pair(pp + 1)
        if n + SCORE_LOOKAHEAD < len(tiles):
            issue_scores(n + SCORE_LOOKAHEAD)
        if n < len(tiles):
            softmax_tile(n)
        if n >= PV_DELAY:
            weighted_values(n - PV_DELAY)


def _attention(kaux, qaux, k, qt, vt):
    step_w = PAIRS_PER_STEP * PAIR_W
    tok = pl.BlockSpec((SEQ, step_w), lambda b, p: (b, p))
    feat = pl.BlockSpec((1, step_w, SEQ), lambda b, p: (b, p, 0))
    return pl.pallas_call(
        _attn_kernel,
        grid=(BATCH, ATTN_W // step_w),
        in_specs=[_whole(kaux)[1], _whole(qaux)[1], tok, feat, feat],
        out_specs=tok,
        out_shape=jax.ShapeDtypeStruct((TOKENS, ATTN_W), BF16),
        scratch_shapes=[pltpu.VMEM((PAIRS_PER_STEP, 2, SEQ, PAIR_W), BF16),
                        pltpu.VMEM((PAIRS_PER_STEP, 2, N_BLK, SEQ), F32),
                        pltpu.VMEM((PAIRS_PER_STEP, 2, HEAD_DIM + SUM_ROWS, SEQ), BF16)],
        compiler_params=_params(2),
        name="moba_attention",
    )(kaux, qaux, k, qt, vt)


CAST_ROWS = 256


def _cast_kernel(*refs):
    n = len(refs) // 2
    for src, dst in zip(refs[:n], refs[n:]):
        dst[...] = src[...].astype(BF16)


def _cast_bf16(layer, *stacked):
    _, rows, cols = stacked[0].shape
    assert all(w.shape == stacked[0].shape for w in stacked) and rows % CAST_ROWS == 0
    return pl.pallas_call(
        _cast_kernel,
        grid=(rows // CAST_ROWS,),
        in_specs=[pl.BlockSpec((None, CAST_ROWS, cols), lambda r: (layer, r, 0))] * len(stacked),
        out_specs=[pl.BlockSpec((CAST_ROWS, cols), lambda r: (r, 0))] * len(stacked),
        out_shape=[jax.ShapeDtypeStruct((rows, cols), BF16) for _ in stacked],
        compiler_params=_params(1),
        name="cast_bf16",
    )(*stacked)


def _transpose_cast_kernel(w_ref, o_ref):
    o_ref[...] = w_ref[...].T.astype(BF16)


def _inproj_weights_transposed(w_in, layer):
    return pl.pallas_call(
        _transpose_cast_kernel,
        grid=(4,),
        in_specs=[pl.BlockSpec((None, D_MODEL, ATTN_W), lambda r: (layer, 0, r + jnp.minimum(r, 1)))],
        out_specs=pl.BlockSpec((ATTN_W, D_MODEL), lambda r: (r, 0)),
        out_shape=jax.ShapeDtypeStruct((4 * ATTN_W, D_MODEL), BF16),
        compiler_params=_params(1),
        name="inproj_weights_t",
    )(w_in)


def kernel(x, p, ffn1_pre_norm, ffn1_w_gate, ffn1_w_up, ffn1_w_down, ffn1_post_norm, mix_pre_norm, w_in, gmlp_ln_g, gmlp_ln_b, gmlp_w_s, gmlp_b_s, attn_out_norm, gmlp_out_norm, w_out, mix_post_norm, ffn2_pre_norm, ffn2_w_gate, ffn2_w_up, ffn2_w_down, ffn2_post_norm, ple_pre_norm, ple_w_gate, ple_w_proj, ple_post_norm):
    assert x.shape == (BATCH, SEQ, D_MODEL) and p.shape == (DEPTH, BATCH, SEQ, PLE_DIM)
    assert ATTN_W == GMLP_W and w_in.shape == (DEPTH, D_MODEL, 5 * ATTN_W)
    kaux, qaux = _attention_constants()
    h = x.reshape(TOKENS, D_MODEL)
    p = p.reshape(DEPTH, TOKENS, PLE_DIM)

    def col(v):
        return jnp.broadcast_to(v[:, :, None], (*v.shape, CHUNK))

    gains = jnp.stack([ffn1_pre_norm, ffn1_post_norm, mix_pre_norm, mix_post_norm,
                       ffn2_pre_norm, ffn2_post_norm, ple_pre_norm, ple_post_norm], axis=1)
    assert gains.shape == (DEPTH, N_GAINS, D_MODEL)
    ln_g, ln_b, gn = col(gmlp_ln_g), col(gmlp_ln_b), col(gmlp_out_norm)
    attn_gain = attn_out_norm.reshape(DEPTH, 1, ATTN_W)

    f1g, f1u = _cast_bf16(0, ffn1_w_gate, ffn1_w_up)
    (f1d,) = _cast_bf16(0, ffn1_w_down)
    w_t = _inproj_weights_transposed(w_in, 0)
    w_k = w_in[0, :, ATTN_W:2 * ATTN_W].astype(BF16)

    for i in range(DEPTH):
        def layer(w):
            return _layer_operand(w, i)

        post_weights = [_side_cast(w, i) for w in (ffn2_w_gate, ffn2_w_up, ffn2_w_down, w_out, ple_w_gate, ple_w_proj)]
        gmlp = (layer(gmlp_w_s), layer(ln_g), layer(ln_b), layer(gmlp_b_s), layer(gn))
        h, k, qt, vt, gmlp_n, f2g, f2u, f2d, wo, pwg, pwp = _pre_attention(
            h, layer(gains), (_whole(f1g), _whole(f1u), _whole(f1d)), (_whole(w_k), _whole(w_t)), gmlp,
            post_weights)
        attn = _attention(kaux, qaux, k, qt, vt)
        next_weights = [] if i + 1 == DEPTH else [
            _side_cast(ffn1_w_gate, i + 1), _side_cast(ffn1_w_up, i + 1), _side_cast(ffn1_w_down, i + 1),
            _side_cast(w_in, i + 1, cols=ATTN_W, col_block=1), _side_cast_inproj_transposed(w_in, i + 1)]
        outproj = (layer(attn_gain), _whole(wo, (ATTN_W, D_MODEL), (0, 0)), _whole(wo, (GMLP_W, D_MODEL), (1, 0)))
        h, *next_bf16 = _post_attention(h, attn, gmlp_n, p, i, layer(gains), outproj,
                                        (_whole(f2g), _whole(f2u), _whole(f2d)), (_whole(pwg), _whole(pwp)),
                                        next_weights)
        if next_bf16:
            f1g, f1u, f1d, w_k, w_t = next_bf16
    return h.reshape(BATCH, SEQ, D_MODEL)
```

```python
import functools
from typing import NamedTuple

import numpy as np
import jax
import jax.numpy as jnp
from jax import lax
from jax.experimental import pallas as pl
from jax.experimental.pallas import tpu as pltpu

D_MODEL = 1024
BATCH = 16
SEQ = 2048
DEPTH = 2
HEAD_DIM = 64
N_HEADS = 8
N_GROUPS = 8
ATTN_W = N_HEADS * HEAD_DIM
GMLP_W = N_GROUPS * HEAD_DIM
BLK = 256
N_BLK = SEQ // BLK
TOP_K = 3
CHUNK = 128
D_FF = 2816
PLE_DIM = 256
RMS_EPS = 1e-6
LN_EPS = 1e-5
NEG = -1e30
TOKENS = BATCH * SEQ

LANES = 128
BF16_SUBLANES = 16
PAIR_W = 2 * HEAD_DIM
TM = 512
SUB = 256
FF_CHUNK = 256
VMEM_LIMIT = 56 * 1024 * 1024

F32 = jnp.float32
BF16 = jnp.bfloat16


def _rms(x, g):
    return x * lax.rsqrt(jnp.mean(x * x, axis=-1, keepdims=True) + RMS_EPS) * g


def _gelu(x):
    return 0.5 * x * (1.0 + lax.erf(x * np.float32(np.sqrt(0.5))))


def _dot(a, b):
    return jnp.dot(a, b, preferred_element_type=F32)


def _dot_nt(a, b):
    return lax.dot_general(a, b, (((1,), (1,)), ((), ())), preferred_element_type=F32)


def _layer_operand(stacked, layer):
    block = stacked.shape[1:]
    spec = pl.BlockSpec((None, *block), lambda *_: (layer, *(0,) * len(block)), pipeline_mode=pl.Buffered(1))
    return stacked, spec


def _whole(array, block=None, index=None):
    block = array.shape if block is None else block
    index = (0,) * len(block) if index is None else index
    return array, pl.BlockSpec(block, lambda *_: index, pipeline_mode=pl.Buffered(1))


def _params(n_axes):
    return pltpu.CompilerParams(
        dimension_semantics=("arbitrary",) * n_axes, vmem_limit_bytes=VMEM_LIMIT)


def _ffn_up(hn, chunks, wg_ref, wu_ref, a_ref):
    for c in chunks:
        sl = slice(c * FF_CHUNK, (c + 1) * FF_CHUNK)
        g = _dot(hn, wg_ref[:, sl])
        u = _dot(hn, wu_ref[:, sl])
        a_ref[:, sl] = (g * jax.nn.sigmoid(g) * u).astype(BF16)


def _ffn_pair(hs, gpre, gpost, wg_ref, wu_ref, wd_ref, a_ref, before_second_down):
    n_chunks = D_FF // FF_CHUNK
    hn0, hn1 = [_rms(h, gpre).astype(BF16) for h in hs]
    _ffn_up(hn0, range(n_chunks), wg_ref, wu_ref, a_ref.at[0])
    _ffn_up(hn1, range(FFN_AHEAD), wg_ref, wu_ref, a_ref.at[1])
    out0 = hs[0] + 0.5 * _rms(_dot(a_ref[0], wd_ref[...]), gpost)
    _ffn_up(hn1, range(FFN_AHEAD, n_chunks), wg_ref, wu_ref, a_ref.at[1])
    filler = before_second_down(out0)
    out1 = hs[1] + 0.5 * _rms(_dot(a_ref[1], wd_ref[...]), gpost)
    return out0, out1, filler


def _feature_major(wt_ref, r, hn):
    return _dot_nt(wt_ref[r * ATTN_W:(r + 1) * ATTN_W, :], hn)


def _inproj_gates(hn, wt_ref):
    gv = _gelu(_feature_major(wt_ref, 3, hn))
    return _gelu(_feature_major(wt_ref, 2, hn)), gv


def _inproj_qkv(hn, rows, wk_ref, wt_ref, k_ref, qt_ref, vt_ref):
    k_ref[rows, :] = _dot(hn, wk_ref[...]).astype(BF16)
    qt_ref[0, :, rows] = _feature_major(wt_ref, 0, hn).astype(BF16)
    vt_ref[0, :, rows] = _feature_major(wt_ref, 1, hn).astype(BF16)


def _gmlp_norm(gv, lng_ref, lnb_ref, vn_ref):
    for c in range(SUB // CHUNK):
        v = gv[:, c * CHUNK:(c + 1) * CHUNK].reshape(N_GROUPS, HEAD_DIM, CHUNK)
        mu = jnp.mean(v, axis=1, keepdims=True)
        var = jnp.mean(jnp.square(v - mu), axis=1, keepdims=True)
        vn = ((v - mu) * lax.rsqrt(var + LN_EPS)).reshape(GMLP_W, CHUNK)
        vn = (vn * lng_ref[...] + lnb_ref[...]).astype(BF16).reshape(N_GROUPS, HEAD_DIM, CHUNK)
        vn_ref[:, c * HEAD_DIM:(c + 1) * HEAD_DIM, :] = vn


def _gmlp_mix(gu, rows, w_causal, bs_ref, gn_ref, o_ref, vn_ref, gated_ref):
    n_chunks = SUB // CHUNK
    ssq = jnp.zeros((n_chunks, 1, CHUNK), F32)
    for g in range(N_GROUPS):
        mixed = _dot_nt(vn_ref[g], w_causal[g]) + bs_ref[g:g + 1, :]
        feat = slice(g * HEAD_DIM, (g + 1) * HEAD_DIM)
        u = jnp.concatenate([gu[feat, c * CHUNK:(c + 1) * CHUNK] for c in range(n_chunks)], axis=0)
        gated = (u * mixed).reshape(n_chunks, HEAD_DIM, CHUNK)
        ssq = ssq + jnp.sum(gated * gated, axis=1, keepdims=True)
        gated_ref[:, feat, :] = gated
    inv = lax.rsqrt(ssq * (1.0 / GMLP_W) + RMS_EPS)
    o_ref[rows, :] = jnp.concatenate(
        [(gated_ref[c] * inv[c] * gn_ref[...]).T for c in range(n_chunks)], axis=0).astype(BF16)


def _outproj_stage(h, rows, attn_ref, gmlp_ref, gpost, an_ref, woa_ref, wog_ref):
    attn_n = _rms(attn_ref[rows, :].astype(F32), an_ref[...]).astype(BF16)
    m = _dot(attn_n, woa_ref[...]) + _dot(gmlp_ref[rows, :], wog_ref[...])
    return h + _rms(m, gpost)


def _ple_stage(h, p_proj, gpre, gpost, wg_ref):
    hn = _rms(h, gpre).astype(BF16)
    gate = jax.nn.sigmoid(_dot(hn, wg_ref[...]))
    return h + _rms(gate * p_proj, gpost)


(G_FFN1_PRE, G_FFN1_POST, G_MIX_PRE, G_MIX_POST, G_FFN2_PRE, G_FFN2_POST, G_PLE_PRE, G_PLE_POST) = range(8)
N_GAINS = 8
N_FFN_REFS = 3
N_INPROJ_REFS = 2
N_GMLP_REFS = 5
N_OUTPROJ_REFS = 3
N_PLE_REFS = 2
SUB_TILES = [slice(r * SUB, (r + 1) * SUB) for r in range(TM // SUB)]
assert len(SUB_TILES) == 2
FFN_AHEAD = 2


def _split(refs, *sizes):
    groups, at = [], 0
    for n in sizes:
        groups.append(refs[at:at + n])
        at += n
    return (*groups, refs[at:])


def _gain(gains_ref, row):
    return gains_ref[row:row + 1, :]


class _SideCast(NamedTuple):
    source: jax.Array
    in_spec: pl.BlockSpec
    out_shape: jax.ShapeDtypeStruct
    out_spec: pl.BlockSpec
    transpose: bool


def _side_cast(stacked, layer, cols=None, col_block=0):
    _, n_rows, n_cols = stacked.shape
    cols = n_cols if cols is None else cols
    steps = max(s for s in range(1, TOKENS // TM + 1)
                if n_rows % s == 0 and (n_rows // s) % BF16_SUBLANES == 0)
    rows = n_rows // steps
    return _SideCast(
        stacked,
        pl.BlockSpec((None, rows, cols), lambda i: (layer, jnp.minimum(i, steps - 1), col_block)),
        jax.ShapeDtypeStruct((n_rows, cols), BF16),
        pl.BlockSpec((rows, cols), lambda i: (jnp.minimum(i, steps - 1), 0)),
        False)


def _side_cast_inproj_transposed(w_in, layer):
    k_first, k_blocks = ATTN_W // LANES, ATTN_W // LANES
    col_blocks = 4 * ATTN_W // LANES
    row_blocks = (TOKENS // TM) // col_blocks
    rows = D_MODEL // row_blocks
    assert row_blocks >= 1 and rows % LANES == 0

    def tile(i):
        c = jnp.minimum(i // row_blocks, col_blocks - 1)
        return i % row_blocks, c, c + jnp.where(c >= k_first, k_blocks, 0)

    return _SideCast(
        w_in,
        pl.BlockSpec((None, rows, LANES), lambda i: (layer, tile(i)[0], tile(i)[2])),
        jax.ShapeDtypeStruct((4 * ATTN_W, D_MODEL), BF16),
        pl.BlockSpec((LANES, rows), lambda i: (tile(i)[1], tile(i)[0])),
        True)


def _run_side_casts(transposes, src_refs, dst_refs):
    for transpose, src, dst in zip(transposes, src_refs, dst_refs):
        x = src[...]
        dst[...] = (x.T if transpose else x).astype(BF16)


def _pre_attention_kernel(side, h_ref, gains_ref, *refs):
    (ffn, (wk_ref, wt_ref), (ws_ref, lng_ref, lnb_ref, bs_ref, gn_ref), side_in,
     (h1_ref, k_ref, qt_ref, vt_ref, gmlp_ref), side_out, (a_ref, vn_ref, gated_ref)) = _split(
        refs, N_FFN_REFS, N_INPROJ_REFS, N_GMLP_REFS, len(side), 5, len(side))
    t_idx = lax.broadcasted_iota(jnp.int32, (CHUNK, CHUNK), 0)
    s_idx = lax.broadcasted_iota(jnp.int32, (CHUNK, CHUNK), 1)
    w_causal = jnp.where(s_idx <= t_idx, ws_ref[...], 0.0).astype(BF16)

    def project_gates(r, h1):
        h1_ref[SUB_TILES[r], :] = h1
        hn = _rms(h1, _gain(gains_ref, G_MIX_PRE)).astype(BF16)
        gu, gv = _inproj_gates(hn, wt_ref)
        _gmlp_norm(gv, lng_ref, lnb_ref, vn_ref.at[r])
        return hn, gu

    def mix(r, gu):
        _gmlp_mix(gu, SUB_TILES[r], w_causal, bs_ref, gn_ref, gmlp_ref, vn_ref.at[r], gated_ref.at[r])

    def project_qkv(r, hn):
        _inproj_qkv(hn, SUB_TILES[r], wk_ref, wt_ref, k_ref, qt_ref, vt_ref)

    _, h1_second, (hn0, gu0) = _ffn_pair(
        [h_ref[rows, :] for rows in SUB_TILES], _gain(gains_ref, G_FFN1_PRE), _gain(gains_ref, G_FFN1_POST),
        *ffn, a_ref, before_second_down=functools.partial(project_gates, 0))
    _run_side_casts(side, side_in, side_out)
    project_qkv(0, hn0)
    hn1, gu1 = project_gates(1, h1_second)
    mix(0, gu0)
    project_qkv(1, hn1)
    mix(1, gu1)


def _pre_attention(h, gains, ffn, inproj, gmlp, side):
    tiles_per_seq = SEQ // TM
    row = pl.BlockSpec((TM, D_MODEL), lambda i: (i, 0))
    tok = pl.BlockSpec((TM, ATTN_W), lambda i: (i, 0))
    feat = pl.BlockSpec((1, ATTN_W, TM), lambda i: (i // tiles_per_seq, 0, i % tiles_per_seq))
    tok_shape = jax.ShapeDtypeStruct((TOKENS, ATTN_W), BF16)
    feat_shape = jax.ShapeDtypeStruct((BATCH, ATTN_W, SEQ), BF16)
    assert len(ffn) == N_FFN_REFS and len(inproj) == N_INPROJ_REFS and len(gmlp) == N_GMLP_REFS
    operands, specs = zip(gains, *ffn, *inproj, *gmlp)
    return pl.pallas_call(
        functools.partial(_pre_attention_kernel, [c.transpose for c in side]),
        grid=(TOKENS // TM,),
        in_specs=[row, *specs, *[c.in_spec for c in side]],
        out_specs=[row, tok, feat, feat, tok, *[c.out_spec for c in side]],
        out_shape=[jax.ShapeDtypeStruct((TOKENS, D_MODEL), F32), tok_shape, feat_shape, feat_shape,
                   jax.ShapeDtypeStruct((TOKENS, GMLP_W), BF16), *[c.out_shape for c in side]],
        scratch_shapes=[pltpu.VMEM((TM // SUB, SUB, D_FF), BF16),
                        pltpu.VMEM((TM // SUB, N_GROUPS, (SUB // CHUNK) * HEAD_DIM, CHUNK), BF16),
                        pltpu.VMEM((TM // SUB, SUB // CHUNK, GMLP_W, CHUNK), F32)],
        compiler_params=_params(1),
        name="ffn1_inproj_gmlp",
    )(h, *operands, *[c.source for c in side])


def _post_attention_kernel(side, h_ref, attn_ref, gmlp_ref, p_ref, gains_ref, *refs):
    outproj, ffn, ple, side_in, (o_ref,), side_out, (a_ref,) = _split(
        refs, N_OUTPROJ_REFS, N_FFN_REFS, N_PLE_REFS, len(side), 1, len(side))
    hs = [_outproj_stage(h_ref[rows, :], rows, attn_ref, gmlp_ref, _gain(gains_ref, G_MIX_POST), *outproj)
          for rows in SUB_TILES]
    wg_ref, wp_ref = ple

    def project_p(_):
        return [_dot(p_ref[rows, :].astype(BF16), wp_ref[...]) for rows in SUB_TILES]

    *hs, p_proj = _ffn_pair(hs, _gain(gains_ref, G_FFN2_PRE), _gain(gains_ref, G_FFN2_POST), *ffn, a_ref,
                            before_second_down=project_p)
    _run_side_casts(side, side_in, side_out)
    for rows, h, pp in zip(SUB_TILES, hs, p_proj):
        o_ref[rows, :] = _ple_stage(h, pp, _gain(gains_ref, G_PLE_PRE), _gain(gains_ref, G_PLE_POST), wg_ref)


def _post_attention(h, attn, gmlp, p, layer, gains, outproj, ffn, ple, side):
    row = pl.BlockSpec((TM, D_MODEL), lambda i: (i, 0))
    tok = pl.BlockSpec((TM, ATTN_W), lambda i: (i, 0))
    assert len(outproj) == N_OUTPROJ_REFS and len(ffn) == N_FFN_REFS and len(ple) == N_PLE_REFS
    operands, specs = zip(gains, *outproj, *ffn, *ple)
    return pl.pallas_call(
        functools.partial(_post_attention_kernel, [c.transpose for c in side]),
        grid=(TOKENS // TM,),
        in_specs=[row, tok, tok, pl.BlockSpec((None, TM, PLE_DIM), lambda i: (layer, i, 0)), *specs,
                  *[c.in_spec for c in side]],
        out_specs=[row, *[c.out_spec for c in side]],
        out_shape=[jax.ShapeDtypeStruct((TOKENS, D_MODEL), F32), *[c.out_shape for c in side]],
        scratch_shapes=[pltpu.VMEM((TM // SUB, SUB, D_FF), BF16)],
        compiler_params=_params(1),
        name="outproj_ffn2_ple",
    )(h, attn, gmlp, p, *operands, *[c.source for c in side])


SPLIT = 4
POS_ROWS = 4 * SPLIT
CHOICE_ROWS = N_BLK
AUX_CHOICE = POS_ROWS
SCORE_LOOKAHEAD = 4
PV_DELAY = 1
PAIRS_PER_STEP = 4
SUM_ROWS = 16
LOG2E = float(np.log2(np.e))


def _split_bf16(x):
    terms, rest = [], x
    for _ in range(SPLIT):
        t = rest.astype(BF16).astype(np.float64)
        terms.append(t)
        rest = rest - t
    assert np.max(np.abs(rest)) <= 1e-7 * max(1.0, np.max(np.abs(x)))
    return terms


def _attention_constants():
    start = 2.0 ** (-8.0 / N_HEADS)
    slopes = np.array([start ** (i + 1) for i in range(N_HEADS)], dtype=np.float32).astype(np.float64)
    pos = np.arange(SEQ)
    local = (pos % BLK).astype(np.float64)
    blk = pos // BLK
    kaux = np.zeros((N_HEADS, SEQ, PAIR_W), np.float64)
    qaux = np.zeros((N_HEADS, POS_ROWS, SEQ), np.float64)
    for h in range(N_HEADS):
        base = (1 - h % 2) * HEAD_DIM
        c = LOG2E * slopes[h]
        fine, coarse = _split_bf16(c * local), _split_bf16(c * BLK * blk)
        for n in range(SPLIT):
            kaux[h, :, base + n] = 1.0
            qaux[h, n] = -fine[n]
            kaux[h, :, base + SPLIT + n] = fine[n]
            qaux[h, SPLIT + n] = 1.0
            kaux[h, :, base + 2 * SPLIT + n] = 1.0
            qaux[h, 2 * SPLIT + n] = -coarse[n]
            kaux[h, :, base + 3 * SPLIT + n] = coarse[n]
            qaux[h, 3 * SPLIT + n] = 1.0
        kaux[h, pos, base + AUX_CHOICE + blk] = 1.0
    assert np.array_equal(kaux.astype(BF16).astype(np.float64), kaux)
    assert np.array_equal(qaux.astype(BF16).astype(np.float64), qaux)
    return jnp.asarray(kaux.astype(BF16)), jnp.asarray(qaux.astype(np.float32))


def _attn_kernel(kaux_ref, qaux_ref, k_ref, qt_ref, vt_ref, o_ref, kaug_ref, choice_ref, vaug_ref):
    lane = lax.broadcasted_iota(jnp.int32, (1, PAIR_W), 1)
    blk_of_row = lax.broadcasted_iota(jnp.int32, (N_BLK, SEQ), 0)
    blk_of_query = lax.broadcasted_iota(jnp.int32, (N_BLK, SEQ), 1) // BLK
    in_past = blk_of_row < blk_of_query
    first_pair = pl.program_id(1) * PAIRS_PER_STEP

    def pair_slice(pp):
        return slice(pp * PAIR_W, (pp + 1) * PAIR_W)

    def prepare_pair(pp):
        k_all = k_ref[:, pair_slice(pp)]
        q_all = qt_ref[0, pair_slice(pp), :]
        k_mean = jnp.mean(k_all.astype(F32).reshape(N_BLK, BLK, PAIR_W), axis=1)
        for a in range(2):
            head = 2 * (first_pair + pp) + a
            in_head = (lane >= a * HEAD_DIM) & (lane < (a + 1) * HEAD_DIM)
            kaug_ref[pp, a] = jnp.where(in_head, k_all, kaux_ref[head])
            km = jnp.where(in_head, k_mean, 0.0)
            km_hi = km.astype(BF16)
            km_lo = (km - km_hi.astype(F32)).astype(BF16)
            gate = _dot(km_hi, q_all) + _dot(km_lo, q_all)
            gate = jnp.where(in_past, gate, NEG)
            rank = jnp.zeros((N_BLK, SEQ), jnp.int32)
            for other in range(N_BLK):
                row = gate[other:other + 1, :]
                beats = (row > gate) | ((row == gate) & (other < blk_of_row))
                rank = rank + beats.astype(jnp.int32)
            chosen = (rank < TOP_K) & (gate > 0.5 * NEG)
            choice_ref[pp, a] = jnp.where(in_past & jnp.logical_not(chosen), NEG, 0.0)
            vaug_ref[pp, a, :HEAD_DIM, :] = vt_ref[0, pl.ds(pp * PAIR_W + a * HEAD_DIM, HEAD_DIM), :]
            vaug_ref[pp, a, HEAD_DIM:, :] = jnp.ones((SUM_ROWS, SEQ), BF16)

    key_idx = lax.broadcasted_iota(jnp.int32, (BLK, BLK), 0)
    query_idx = lax.broadcasted_iota(jnp.int32, (BLK, BLK), 1)
    causal = key_idx <= query_idx
    aux_fill = jnp.zeros((HEAD_DIM - POS_ROWS - CHOICE_ROWS, BLK), F32)

    def blk_slice(j):
        return slice(j * BLK, (j + 1) * BLK)

    tiles = [(pp, i, a, j) for pp in range(PAIRS_PER_STEP) for i in range(N_BLK)
             for j in [i] + list(range(i)) for a in range(2)]
    q_augs, scores, probs, running_max, acc_state, outs = {}, {}, {}, {}, {}, {}

    def issue_scores(n):
        pp, i, a, j = tiles[n]
        if (pp, i, a) not in q_augs:
            head = 2 * (first_pair + pp) + a
            feat = pl.ds(pp * PAIR_W + a * HEAD_DIM, HEAD_DIM)
            q_head = qt_ref[0, feat, blk_slice(i)].astype(F32) * (LOG2E * HEAD_DIM ** -0.5)
            aux = jnp.concatenate(
                [qaux_ref[head, :, blk_slice(i)], choice_ref[pp, a, :, blk_slice(i)], aux_fill], axis=0)
            q_augs[pp, i, a] = jnp.concatenate(
                [q_head, aux] if a == 0 else [aux, q_head], axis=0).astype(BF16)
        scores[n] = _dot(kaug_ref[pp, a, blk_slice(j), :], q_augs[pp, i, a])

    def softmax_tile(n):
        pp, i, a, j = tiles[n]
        s = scores.pop(n)
        if j == i:
            s = jnp.where(causal, s, NEG)
            m = jnp.max(s, axis=0, keepdims=True)
            alpha = None
        else:
            m_old = running_max[pp, i, a]
            m = jnp.maximum(m_old, jnp.max(s, axis=0, keepdims=True))
            alpha = jnp.exp2(m_old - m)
        running_max[pp, i, a] = m
        probs[n] = (jnp.exp2(s - m).astype(BF16), alpha)

    def weighted_values(n):
        pp, i, a, j = tiles[n]
        p, alpha = probs.pop(n)
        pv = _dot(vaug_ref[pp, a, :, blk_slice(j)], p)
        acc = pv if alpha is None else alpha * acc_state[pp, i, a] + pv
        acc_state[pp, i, a] = acc
        if j == (i - 1 if i > 0 else 0):
            outs.setdefault((pp, i), []).append(acc[:HEAD_DIM] / acc[HEAD_DIM:HEAD_DIM + 1])
            del acc_state[pp, i, a]
            if a == 1:
                o_ref[blk_slice(i), pair_slice(pp)] = jnp.concatenate(outs.pop((pp, i)), axis=0).T.astype(BF16)

    prepare_pair(0)
    for n in range(min(SCORE_LOOKAHEAD, len(tiles))):
        issue_scores(n)
    for n in range(len(tiles) + PV_DELAY):
        if n < len(tiles):
            pp, i, a, j = tiles[n]
            if (i, a, j) == (N_BLK - 1, 0, N_BLK - 1) and pp + 1 < PAIRS_PER_STEP:
                prepare_pair(pp + 1)
        if n + SCORE_LOOKAHEAD < len(tiles):
            issue_scores(n + SCORE_LOOKAHEAD)
        if n < len(tiles):
            softmax_tile(n)
        if n >= PV_DELAY:
            weighted_values(n - PV_DELAY)


def _attention(kaux, qaux, k, qt, vt):
    step_w = PAIRS_PER_STEP * PAIR_W
    tok = pl.BlockSpec((SEQ, step_w), lambda b, p: (b, p))
    feat = pl.BlockSpec((1, step_w, SEQ), lambda b, p: (b, p, 0))
    return pl.pallas_call(
        _attn_kernel,
        grid=(BATCH, ATTN_W // step_w),
        in_specs=[_whole(kaux)[1], _whole(qaux)[1], tok, feat, feat],
        out_specs=tok,
        out_shape=jax.ShapeDtypeStruct((TOKENS, ATTN_W), BF16),
        scratch_shapes=[pltpu.VMEM((PAIRS_PER_STEP, 2, SEQ, PAIR_W), BF16),
                        pltpu.VMEM((PAIRS_PER_STEP, 2, N_BLK, SEQ), F32),
                        pltpu.VMEM((PAIRS_PER_STEP, 2, HEAD_DIM + SUM_ROWS, SEQ), BF16)],
        compiler_params=_params(2),
        name="moba_attention",
    )(kaux, qaux, k, qt, vt)


CAST_ROWS = 256


def _cast_kernel(*refs):
    n = len(refs) // 2
    for src, dst in zip(refs[:n], refs[n:]):
        dst[...] = src[...].astype(BF16)


def _cast_bf16(layer, *stacked):
    _, rows, cols = stacked[0].shape
    assert all(w.shape == stacked[0].shape for w in stacked) and rows % CAST_ROWS == 0
    return pl.pallas_call(
        _cast_kernel,
        grid=(rows // CAST_ROWS,),
        in_specs=[pl.BlockSpec((None, CAST_ROWS, cols), lambda r: (layer, r, 0))] * len(stacked),
        out_specs=[pl.BlockSpec((CAST_ROWS, cols), lambda r: (r, 0))] * len(stacked),
        out_shape=[jax.ShapeDtypeStruct((rows, cols), BF16) for _ in stacked],
        compiler_params=_params(1),
        name="cast_bf16",
    )(*stacked)


def _transpose_cast_kernel(w_ref, o_ref):
    o_ref[...] = w_ref[...].T.astype(BF16)


def _inproj_weights_transposed(w_in, layer):
    return pl.pallas_call(
        _transpose_cast_kernel,
        grid=(4,),
        in_specs=[pl.BlockSpec((None, D_MODEL, ATTN_W), lambda r: (layer, 0, r + jnp.minimum(r, 1)))],
        out_specs=pl.BlockSpec((ATTN_W, D_MODEL), lambda r: (r, 0)),
        out_shape=jax.ShapeDtypeStruct((4 * ATTN_W, D_MODEL), BF16),
        compiler_params=_params(1),
        name="inproj_weights_t",
    )(w_in)


def kernel(x, p, ffn1_pre_norm, ffn1_w_gate, ffn1_w_up, ffn1_w_down, ffn1_post_norm, mix_pre_norm, w_in, gmlp_ln_g, gmlp_ln_b, gmlp_w_s, gmlp_b_s, attn_out_norm, gmlp_out_norm, w_out, mix_post_norm, ffn2_pre_norm, ffn2_w_gate, ffn2_w_up, ffn2_w_down, ffn2_post_norm, ple_pre_norm, ple_w_gate, ple_w_proj, ple_post_norm):
    assert x.shape == (BATCH, SEQ, D_MODEL) and p.shape == (DEPTH, BATCH, SEQ, PLE_DIM)
    assert ATTN_W == GMLP_W and w_in.shape == (DEPTH, D_MODEL, 5 * ATTN_W)
    kaux, qaux = _attention_constants()
    h = x.reshape(TOKENS, D_MODEL)
    p = p.reshape(DEPTH, TOKENS, PLE_DIM)

    def col(v):
        return jnp.broadcast_to(v[:, :, None], (*v.shape, CHUNK))

    gains = jnp.stack([ffn1_pre_norm, ffn1_post_norm, mix_pre_norm, mix_post_norm,
                       ffn2_pre_norm, ffn2_post_norm, ple_pre_norm, ple_post_norm], axis=1)
    assert gains.shape == (DEPTH, N_GAINS, D_MODEL)
    ln_g, ln_b, gn = col(gmlp_ln_g), col(gmlp_ln_b), col(gmlp_out_norm)
    attn_gain = attn_out_norm.reshape(DEPTH, 1, ATTN_W)

    f1g, f1u = _cast_bf16(0, ffn1_w_gate, ffn1_w_up)
    (f1d,) = _cast_bf16(0, ffn1_w_down)
    w_t = _inproj_weights_transposed(w_in, 0)
    w_k = w_in[0, :, ATTN_W:2 * ATTN_W].astype(BF16)

    for i in range(DEPTH):
        def layer(w):
            return _layer_operand(w, i)

        post_weights = [_side_cast(w, i) for w in (ffn2_w_gate, ffn2_w_up, ffn2_w_down, w_out, ple_w_gate, ple_w_proj)]
        gmlp = (layer(gmlp_w_s), layer(ln_g), layer(ln_b), layer(gmlp_b_s), layer(gn))
        h, k, qt, vt, gmlp_n, f2g, f2u, f2d, wo, pwg, pwp = _pre_attention(
            h, layer(gains), (_whole(f1g), _whole(f1u), _whole(f1d)), (_whole(w_k), _whole(w_t)), gmlp,
            post_weights)
        attn = _attention(kaux, qaux, k, qt, vt)
        next_weights = [] if i + 1 == DEPTH else [
            _side_cast(ffn1_w_gate, i + 1), _side_cast(ffn1_w_up, i + 1), _side_cast(ffn1_w_down, i + 1),
            _side_cast(w_in, i + 1, cols=ATTN_W, col_block=1), _side_cast_inproj_transposed(w_in, i + 1)]
        outproj = (layer(attn_gain), _whole(wo, (ATTN_W, D_MODEL), (0, 0)), _whole(wo, (GMLP_W, D_MODEL), (1, 0)))
        h, *next_bf16 = _post_attention(h, attn, gmlp_n, p, i, layer(gains), outproj,
                                        (_whole(f2g), _whole(f2u), _whole(f2d)), (_whole(pwg), _whole(pwp)),
                                        next_weights)
        if next_bf16:
            f1g, f1u, f1d, w_k, w_t = next_bf16
    return h.reshape(BATCH, SEQ, D_MODEL)
```
